```python
import math
import jax
import jax.numpy as jnp
from jax import lax
import numpy as np

D_MODEL = 1024
BATCH = 8
SEQ = 2048
DEPTH = 4

GRID_W = 64
CTX_LEN = 256
EPS = 1e-6
N_MOD = 9

POOL_WIDTH = 256
POOL_WINDOWS = (2, 4, 8, 16)
POOL_GROUP = POOL_WIDTH // len(POOL_WINDOWS)

HYENA_WIDTH = 256
HYENA_ORDER = 2
HYENA_EMB_DIM = 33
HYENA_FILTER_ORDER = 64
HYENA_DECAY_TARGET = 1e-2
HYENA_FAST_DECAY_PCT = 0.3
HYENA_SLOW_DECAY_PCT = 1.5

HEAD_DIM = 64
N_Q_HEADS = 8
N_KV_HEADS = 2
GQA_GROUP = N_Q_HEADS // N_KV_HEADS
ATTN_WIDTH = N_Q_HEADS * HEAD_DIM
KV_WIDTH = N_KV_HEADS * HEAD_DIM
AXIS_DIM = HEAD_DIM // 2
ROPE_THETA = 10000.0
Q_BLOCK = 128
ATTN_SCALE = HEAD_DIM ** -0.5

HYENA_OFF = POOL_WIDTH
Q_OFF = HYENA_OFF + (HYENA_ORDER + 1) * HYENA_WIDTH
K_OFF = Q_OFF + ATTN_WIDTH
V_OFF = K_OFF + KV_WIDTH
IN_WIDTH = V_OFF + KV_WIDTH
MIX_WIDTH = POOL_WIDTH + HYENA_WIDTH + ATTN_WIDTH

D_FF = 2816

kernel_name = 'hymba_pool_hyena_gqa_prefix_dit_block'


def rms_norm(x, g):
    xf = x.astype(jnp.float32)
    y = xf * lax.rsqrt(jnp.mean(xf * xf, axis=-1, keepdims=True) + EPS)
    return (y * g.astype(jnp.float32)).astype(x.dtype)


def modulate(h, shift, scale):
    return h * (1 + scale) + shift


def swiglu(h, w_gate, w_up, w_down):
    return (jax.nn.silu(h @ w_gate) * (h @ w_up)) @ w_down


def ffn_half_step(h, g, shift, scale, gate, w_gate, w_up, w_down):
    y = modulate(rms_norm(h, g), shift, scale)
    return h + 0.5 * gate * swiglu(y, w_gate, w_up, w_down)


def multiscale_pool(u, w_pool, pool_scale):
    B, L, _ = u.shape
    uf = u.astype(jnp.float32)
    cs = jnp.concatenate([jnp.zeros((B, 1, POOL_WIDTH), jnp.float32), jnp.cumsum(uf, axis=1)], axis=1)
    t = jnp.arange(L)
    groups = []
    for gi, win in enumerate(POOL_WINDOWS):
        lo = jnp.clip(t - win // 2, 0, L)
        hi = jnp.clip(t - win // 2 + win, 0, L)
        sl = slice(gi * POOL_GROUP, (gi + 1) * POOL_GROUP)
        seg = cs[..., sl]
        window_sum = jnp.take(seg, hi, axis=1) - jnp.take(seg, lo, axis=1)
        count = (hi - lo).astype(jnp.float32)[None, :, None]
        groups.append(window_sum / count - uf[..., sl])
    pooled = jnp.stack(groups, axis=2).astype(u.dtype)
    mixed = jnp.einsum('blgc,gcd->blgd', pooled, w_pool).reshape(B, L, POOL_WIDTH)
    return mixed * pool_scale


def short_conv3(u, w, b):
    up = jnp.pad(u, ((0, 0), (1, 1), (0, 0)))
    return up[:, :-2] * w[0] + up[:, 1:-1] * w[1] + up[:, 2:] * w[2] + b


def hyena_filters(L, f_w1, f_b1, f_w2, f_b2, f_w3, sin_freq):
    f32 = jnp.float32
    t = jnp.linspace(0.0, 1.0, L, dtype=f32)[:, None]
    bands = (HYENA_EMB_DIM - 1) // 2
    w_ang = 2.0 * math.pi * jnp.arange(L, dtype=f32) / L
    freqs = jnp.linspace(1e-4, bands - 1, bands, dtype=f32)
    ang = w_ang[:, None] * freqs[None, :]
    z = jnp.concatenate([t, jnp.cos(ang), -jnp.sin(ang)], axis=-1)
    sf = sin_freq.astype(f32)
    hdn = jnp.sin(sf[0] * (z @ f_w1.astype(f32) + f_b1.astype(f32)))
    hdn = jnp.sin(sf[1] * (hdn @ f_w2.astype(f32) + f_b2.astype(f32)))
    filt = (hdn @ f_w3.astype(f32)).reshape(L, HYENA_ORDER, 2, HYENA_WIDTH)
    max_decay = math.log(HYENA_DECAY_TARGET) / HYENA_FAST_DECAY_PCT
    min_decay = math.log(HYENA_DECAY_TARGET) / HYENA_SLOW_DECAY_PCT
    deltas = jnp.linspace(min_decay, max_decay, HYENA_WIDTH, dtype=f32)
    decay = jnp.exp(-t * jnp.abs(deltas))
    filt = filt * decay[:, None, None, :]
    fwd = filt[:, :, 0]
    bwd = filt[1:, :, 1][::-1]
    k = jnp.concatenate([fwd, jnp.zeros_like(fwd[:1]), bwd], axis=0)
    return k / jnp.sum(jnp.abs(k), axis=0, keepdims=True)


def fft_long_conv(u, k_f):
    L = u.shape[1]
    u_f = jnp.fft.rfft(u, n=2 * L, axis=1)
    return jnp.fft.irfft(u_f * k_f, n=2 * L, axis=1)[:, :L]


def hyena_mixer(proj, conv_w, conv_b, f_w1, f_b1, f_w2, f_b2, f_w3, sin_freq, hy_bias):
    L = proj.shape[1]
    zc = short_conv3(proj, conv_w, conv_b).astype(jnp.float32)
    v, x1, x2 = jnp.split(zc, 3, axis=-1)
    k_f = jnp.fft.rfft(hyena_filters(L, f_w1, f_b1, f_w2, f_b2, f_w3, sin_freq), axis=0)
    hb = hy_bias.astype(jnp.float32)
    y = v
    for o, gate in enumerate((x1, x2)):
        y = gate * (fft_long_conv(y, k_f[:, o]) + hb[o] * y)
    return y.astype(proj.dtype)


def axial_rope_tables(rows):
    f32 = jnp.float32
    row = jnp.repeat(jnp.arange(rows), GRID_W).astype(f32)
    col = jnp.tile(jnp.arange(GRID_W), rows).astype(f32)
    inv_freq = 1.0 / (ROPE_THETA ** (jnp.arange(0, AXIS_DIM, 2, dtype=f32) / AXIS_DIM))
    ang_r = row[:, None] * inv_freq
    ang_c = col[:, None] * inv_freq
    ang = jnp.concatenate([ang_r, ang_r, ang_c, ang_c], axis=-1)
    return jnp.cos(ang), jnp.sin(ang)


def apply_axial_rope(x, cos, sin):
    xf = x.astype(jnp.float32)
    x1, x2, x3, x4 = jnp.split(xf, 4, axis=-1)
    rot = jnp.concatenate([-x2, x1, -x4, x3], axis=-1)
    return (xf * cos[:, None, :] + rot * sin[:, None, :]).astype(x.dtype)


def attn_heads(p, n_heads, g, cos, sin):
    B, S, _ = p.shape
    hd = rms_norm(p.reshape(B, S, n_heads, HEAD_DIM), g)
    if cos is not None:
        hd = apply_axial_rope(hd, cos, sin)
    return hd


def attend(q, k, v):
    s = jnp.einsum('bqkgd,bskd->bkgqs', q, k, preferred_element_type=jnp.float32) * ATTN_SCALE
    p = jax.nn.softmax(s, axis=-1).astype(v.dtype)
    o = jnp.einsum('bkgqs,bskd->bqkgd', p, v)
    return o.reshape(o.shape[0], o.shape[1], ATTN_WIDTH)


def blocked_attention(q, k, v):
    B, L = q.shape[:2]
    qb = jnp.moveaxis(q.reshape(B, L // Q_BLOCK, Q_BLOCK, N_KV_HEADS, GQA_GROUP, HEAD_DIM), 1, 0)
    out = lax.map(lambda qblk: attend(qblk, k, v), qb)
    return jnp.moveaxis(out, 0, 1).reshape(B, L, ATTN_WIDTH)


def setup_inputs(seed: int = 0) -> dict:
    key = jax.random.key(seed)
    ks = jax.random.split(key, 32)
    f32 = jnp.float32

    def nrm(k, shape, scale):
        return jax.random.normal(k, shape, f32) * scale

    return {
        'x': nrm(ks[0], (BATCH, SEQ, D_MODEL), 1.0),
        'c': nrm(ks[1], (BATCH, D_MODEL), 1.0),
        'ctx': nrm(ks[2], (BATCH, CTX_LEN, D_MODEL), 1.0),
        'c_ctx': nrm(ks[3], (D_MODEL,), 1.0),
        'norm_g': 1.0 + nrm(ks[4], (DEPTH, 3, D_MODEL), 0.05),
        'w_mod': nrm(ks[5], (DEPTH, D_MODEL, N_MOD * D_MODEL), D_MODEL ** -0.5),
        'b_mod': nrm(ks[6], (DEPTH, N_MOD * D_MODEL), 0.01),
        'ffn_w_gate': nrm(ks[7], (DEPTH, 2, D_MODEL, D_FF), D_MODEL ** -0.5),
        'ffn_w_up': nrm(ks[8], (DEPTH, 2, D_MODEL, D_FF), D_MODEL ** -0.5),
        'ffn_w_down': nrm(ks[9], (DEPTH, 2, D_FF, D_MODEL), D_FF ** -0.5),
        'w_in': nrm(ks[10], (DEPTH, D_MODEL, IN_WIDTH), D_MODEL ** -0.5),
        'w_out': nrm(ks[11], (DEPTH, MIX_WIDTH, D_MODEL), MIX_WIDTH ** -0.5),
        'pool_w': nrm(ks[12], (DEPTH, len(POOL_WINDOWS), POOL_GROUP, POOL_GROUP), POOL_GROUP ** -0.5),
        'pool_scale': 1.0 + nrm(ks[13], (DEPTH, POOL_WIDTH), 0.1),
        'hyena_conv_w': nrm(ks[14], (DEPTH, 3, (HYENA_ORDER + 1) * HYENA_WIDTH), 3.0 ** -0.5),
        'hyena_conv_b': nrm(ks[15], (DEPTH, (HYENA_ORDER + 1) * HYENA_WIDTH), 0.01),
        'hyena_f_w1': nrm(ks[16], (DEPTH, HYENA_EMB_DIM, HYENA_FILTER_ORDER), HYENA_EMB_DIM ** -0.5),
        'hyena_f_b1': nrm(ks[17], (DEPTH, HYENA_FILTER_ORDER), 0.1),
        'hyena_f_w2': nrm(ks[18], (DEPTH, HYENA_FILTER_ORDER, HYENA_FILTER_ORDER), HYENA_FILTER_ORDER ** -0.5),
        'hyena_f_b2': nrm(ks[19], (DEPTH, HYENA_FILTER_ORDER), 0.1),
        'hyena_f_w3': nrm(ks[20], (DEPTH, HYENA_FILTER_ORDER, HYENA_ORDER * 2 * HYENA_WIDTH), HYENA_FILTER_ORDER ** -0.5),
        'hyena_sin_freq': 1.0 + nrm(ks[21], (DEPTH, 2, HYENA_FILTER_ORDER), 0.1),
        'hyena_bias': nrm(ks[22], (DEPTH, HYENA_ORDER, HYENA_WIDTH), 1.0),
        'q_norm_g': 1.0 + nrm(ks[23], (DEPTH, HEAD_DIM), 0.05),
        'k_norm_g': 1.0 + nrm(ks[24], (DEPTH, HEAD_DIM), 0.05),
    }


def reference(x, c, ctx, c_ctx, norm_g, w_mod, b_mod, ffn_w_gate, ffn_w_up, ffn_w_down,
              w_in, w_out, pool_w, pool_scale, hyena_conv_w, hyena_conv_b,
              hyena_f_w1, hyena_f_b1, hyena_f_w2, hyena_f_b2, hyena_f_w3,
              hyena_sin_freq, hyena_bias, q_norm_g, k_norm_g):
    B, L, D = x.shape
    C = ctx.shape[1]
    rows = L // GRID_W
    cos, sin = axial_rope_tables(rows)
    c_act = jax.nn.silu(c)
    cc_act = jax.nn.silu(c_ctx)
    h, hc = x, ctx
    for l in range(DEPTH):
        last = l == DEPTH - 1
        mx = (c_act @ w_mod[l] + b_mod[l]).reshape(B, N_MOD, 1, D)
        mc = (cc_act @ w_mod[l] + b_mod[l]).reshape(N_MOD, D)
        f1 = (ffn_w_gate[l, 0], ffn_w_up[l, 0], ffn_w_down[l, 0])
        f2 = (ffn_w_gate[l, 1], ffn_w_up[l, 1], ffn_w_down[l, 1])
        hy = (hyena_conv_w[l], hyena_conv_b[l], hyena_f_w1[l], hyena_f_b1[l], hyena_f_w2[l],
              hyena_f_b2[l], hyena_f_w3[l], hyena_sin_freq[l], hyena_bias[l])

        h = ffn_half_step(h, norm_g[l, 0], mx[:, 0], mx[:, 1], mx[:, 2], *f1)
        hc = ffn_half_step(hc, norm_g[l, 0], mc[0], mc[1], mc[2], *f1)

        ux = modulate(rms_norm(h, norm_g[l, 1]), mx[:, 3], mx[:, 4])
        uc = modulate(rms_norm(hc, norm_g[l, 1]), mc[3], mc[4])
        px = ux @ w_in[l]
        pc = uc @ (w_in[l][:, K_OFF:] if last else w_in[l])

        qx = attn_heads(px[..., Q_OFF:K_OFF], N_Q_HEADS, q_norm_g[l], cos, sin)
        qx = qx.reshape(B, L, N_KV_HEADS, GQA_GROUP, HEAD_DIM)
        kx = attn_heads(px[..., K_OFF:V_OFF], N_KV_HEADS, k_norm_g[l], cos, sin)
        vx = px[..., V_OFF:].reshape(B, L, N_KV_HEADS, HEAD_DIM)
        pc_kv = pc[..., -2 * KV_WIDTH:]
        kc = attn_heads(pc_kv[..., :KV_WIDTH], N_KV_HEADS, k_norm_g[l], None, None)
        vc = pc_kv[..., KV_WIDTH:].reshape(B, C, N_KV_HEADS, HEAD_DIM)

        k_all = jnp.concatenate([kx, kc], axis=1)
        v_all = jnp.concatenate([vx, vc], axis=1)
        mix_x = jnp.concatenate([
            multiscale_pool(px[..., :HYENA_OFF], pool_w[l], pool_scale[l]),
            hyena_mixer(px[..., HYENA_OFF:Q_OFF], *hy),
            blocked_attention(qx, k_all, v_all),
        ], axis=-1) @ w_out[l]
        h = h + mx[:, 5] * mix_x

        if not last:
            qc = attn_heads(pc[..., Q_OFF:K_OFF], N_Q_HEADS, q_norm_g[l], None, None)
            qc = qc.reshape(B, C, N_KV_HEADS, GQA_GROUP, HEAD_DIM)
            mix_c = jnp.concatenate([
                multiscale_pool(pc[..., :HYENA_OFF], pool_w[l], pool_scale[l]),
                hyena_mixer(pc[..., HYENA_OFF:Q_OFF], *hy),
                attend(qc, kc, vc),
            ], axis=-1) @ w_out[l]
            hc = hc + mc[5] * mix_c
            hc = ffn_half_step(hc, norm_g[l, 2], mc[6], mc[7], mc[8], *f2)

        h = ffn_half_step(h, norm_g[l, 2], mx[:, 6], mx[:, 7], mx[:, 8], *f2)
    return h
```

```python
import functools
import math

import jax
import jax.numpy as jnp
import numpy as np
from jax import lax
from jax.experimental import pallas as pl
from jax.experimental.pallas import tpu as pltpu

F32 = jnp.float32
BF16 = jnp.bfloat16

GRID_W = 64
EPS = 1e-6
N_MOD = 9
POOL_WIDTH = 256
POOL_WINDOWS = (2, 4, 8, 16)
POOL_GROUP = POOL_WIDTH // len(POOL_WINDOWS)
HYENA_WIDTH = 256
HYENA_ORDER = 2
HYENA_EMB_DIM = 33
HYENA_FILTER_ORDER = 64
HYENA_DECAY_TARGET = 1e-2
HYENA_FAST_DECAY_PCT = 0.3
HYENA_SLOW_DECAY_PCT = 1.5
HEAD_DIM = 64
N_Q_HEADS = 8
N_KV_HEADS = 2
GQA_GROUP = N_Q_HEADS // N_KV_HEADS
ATTN_WIDTH = N_Q_HEADS * HEAD_DIM
KV_WIDTH = N_KV_HEADS * HEAD_DIM
AXIS_DIM = HEAD_DIM // 2
ROPE_THETA = 10000.0
ATTN_SCALE = HEAD_DIM ** -0.5
HYENA_OFF = POOL_WIDTH
Q_OFF = HYENA_OFF + (HYENA_ORDER + 1) * HYENA_WIDTH
K_OFF = Q_OFF + ATTN_WIDTH
V_OFF = K_OFF + KV_WIDTH
IN_WIDTH = V_OFF + KV_WIDTH

LANES = 128
SUBLANES = 8
VMEM_LIMIT_BYTES = 56 * 1024 * 1024

TOKEN_TILE = 512
Q_TILE = 256
DFT_BLOCK = 512
MOD_COL_TILE = 1152
PAD_ROWS = 16


def _params(n_axes):
    return pltpu.CompilerParams(
        dimension_semantics=("arbitrary",) * n_axes, vmem_limit_bytes=VMEM_LIMIT_BYTES)


def _resident(shape):
    zeros = (0,) * len(shape)
    return pl.BlockSpec(shape, lambda *_: zeros, pipeline_mode=pl.Buffered(1))


def _split_bf16(a):
    hi = a.astype(BF16)
    lo = (a - hi.astype(F32)).astype(BF16)
    return hi, lo


def _dot(a, b):
    return jnp.dot(a, b, preferred_element_type=F32)


def _dot3(a, b):
    a_hi, a_lo = _split_bf16(a)
    b_hi, b_lo = _split_bf16(b)
    return _dot(a_hi, b_hi) + _dot(a_lo, b_hi) + _dot(a_hi, b_lo)


def _norm_mod(x, g, shift, scale):
    ms = jnp.mean(x * x, axis=-1, keepdims=True)
    y = x * lax.rsqrt(ms + EPS) * g
    return y * (1.0 + scale) + shift


def _mods_kernel(c_ref, w_ref, b_ref, o_ref):
    rows = c_ref.shape[0]
    cc = c_ref[...]
    act = cc * jax.nn.sigmoid(cc)
    a_hi, a_lo = _split_bf16(act)
    w_hi, w_lo = _split_bf16(w_ref[0])
    r = _dot(jnp.concatenate([a_hi, a_lo], axis=0), w_hi)
    o_ref[0] = r[:rows] + r[rows:] + _dot(a_hi, w_lo) + b_ref[0]


def _mods_call(cvec, w_mod, b_mod):
    depth, d, n = w_mod.shape
    rows = cvec.shape[0]
    tn = MOD_COL_TILE
    return pl.pallas_call(
        _mods_kernel,
        grid=(depth, n // tn),
        in_specs=[
            pl.BlockSpec((rows, d), lambda l, j: (0, 0)),
            pl.BlockSpec((1, d, tn), lambda l, j: (l, 0, j)),
            pl.BlockSpec((1, 1, tn), lambda l, j: (l, 0, j)),
        ],
        out_specs=pl.BlockSpec((1, rows, tn), lambda l, j: (l, 0, j)),
        out_shape=jax.ShapeDtypeStruct((depth, rows, n), F32),
        compiler_params=_params(2),
        name="mods",
    )(cvec, w_mod, b_mod.reshape(depth, 1, n))


FF_CHUNK = 1024


def _ffn_kernel(x_ref, mod_ref, g_ref, wg_ref, wu_ref, wd_ref, o_ref, *, mod_off):
    x = x_ref[0]
    m = mod_ref[0]
    y = _norm_mod(x, g_ref[...], m[mod_off:mod_off + 1], m[mod_off + 1:mod_off + 2]).astype(BF16)
    d_ff = wg_ref.shape[1]
    acc = jnp.zeros(x.shape, F32)
    for lo in range(0, d_ff, FF_CHUNK):
        hi = min(lo + FF_CHUNK, d_ff)
        a = _dot(y, wg_ref[:, lo:hi])
        u = _dot(y, wu_ref[:, lo:hi])
        mid = (a * jax.nn.sigmoid(a) * u).astype(BF16)
        acc = acc + _dot(mid, wd_ref[lo:hi, :])
    o_ref[0] = x + 0.5 * m[mod_off + 2:mod_off + 3] * acc


def _ffn_call(h, mods, g, wg, wu, wd, *, mod_row, mod_off, n_seq=None):
    s_all, lq, d = h.shape
    n_seq = s_all if n_seq is None else n_seq
    tm = min(TOKEN_TILE, lq)
    d_ff = wg.shape[1]
    return pl.pallas_call(
        functools.partial(_ffn_kernel, mod_off=mod_off),
        grid=(n_seq, lq // tm),
        in_specs=[
            pl.BlockSpec((1, tm, d), lambda s, i: (s, i, 0)),
            pl.BlockSpec((1, N_MOD, d), lambda s, i: (mod_row(s), 0, 0)),
            _resident((1, d)),
            _resident((d, d_ff)),
            _resident((d, d_ff)),
            _resident((d_ff, d)),
        ],
        out_specs=pl.BlockSpec((1, tm, d), lambda s, i: (s, i, 0)),
        out_shape=jax.ShapeDtypeStruct((n_seq, lq, d), F32),
        compiler_params=_params(2),
        name="ffn",
    )(h, mods, g.reshape(1, d), wg, wu, wd)


def _head_norm_rope(x, g, seg_mean, cos, sin_a, sin_b):
    sq_hi, sq_lo = _split_bf16(x * x)
    ms = _dot(sq_hi, seg_mean) + _dot(sq_lo, seg_mean)
    y = x * lax.rsqrt(ms + EPS) * g
    return (y * cos + pltpu.roll(y, AXIS_DIM // 2, 1) * sin_a
            + pltpu.roll(y, LANES - AXIS_DIM // 2, 1) * sin_b)


def _inproj_kernel(x_ref, mod_ref, g_ref, w_ref, cos_ref, sa_ref, sb_ref, qg_ref, kg_ref, sm_ref,
                   pool_ref, hy_ref, q_ref, k_ref, v_ref):
    x = x_ref[0]
    m = mod_ref[0]
    y = _norm_mod(x, g_ref[...], m[3:4], m[4:5]).astype(BF16)
    p = _dot(y, w_ref[...])
    pool_ref[0] = p[:, :HYENA_OFF]
    hy_ref[0] = p[:, HYENA_OFF:Q_OFF]
    cos, sa, sb, sm = cos_ref[...], sa_ref[...], sb_ref[...], sm_ref[...]
    qs = []
    for j in range(ATTN_WIDTH // LANES):
        qj = p[:, Q_OFF + j * LANES:Q_OFF + (j + 1) * LANES]
        qs.append(_head_norm_rope(qj, qg_ref[...], sm, cos, sa, sb) * ATTN_SCALE)
    q_ref[0] = jnp.concatenate(qs, axis=1).astype(BF16)
    k_ref[0] = _head_norm_rope(p[:, K_OFF:V_OFF], kg_ref[...], sm, cos, sa, sb).astype(BF16)
    v_ref[0] = p[:, V_OFF:].astype(BF16)


def _inproj_call(h, mods, g, w_in, tables, qg, kg, seg_mean, *, mod_row):
    n_seq, lq, d = h.shape
    tm = min(TOKEN_TILE, lq)
    cos, sa, sb = tables
    tab_spec = pl.BlockSpec((tm, LANES), lambda s, i: (i, 0))
    tok = lambda w: pl.BlockSpec((1, tm, w), lambda s, i: (s, i, 0))
    shp = lambda w, dt: jax.ShapeDtypeStruct((n_seq, lq, w), dt)
    return pl.pallas_call(
        _inproj_kernel,
        grid=(n_seq, lq // tm),
        in_specs=[
            tok(d),
            pl.BlockSpec((1, N_MOD, d), lambda s, i: (mod_row(s), 0, 0)),
            _resident((1, d)),
            _resident((d, IN_WIDTH)),
            tab_spec, tab_spec, tab_spec,
            _resident((1, LANES)),
            _resident((1, LANES)),
            _resident((LANES, LANES)),
        ],
        out_specs=[tok(POOL_WIDTH), tok(Q_OFF - HYENA_OFF), tok(ATTN_WIDTH), tok(KV_WIDTH), tok(KV_WIDTH)],
        out_shape=[shp(POOL_WIDTH, F32), shp(Q_OFF - HYENA_OFF, F32), shp(ATTN_WIDTH, BF16),
                   shp(KV_WIDTH, BF16), shp(KV_WIDTH, BF16)],
        compiler_params=_params(2),
        name="inproj",
    )(h, mods, g.reshape(1, d), w_in, cos, sa, sb, qg, kg, seg_mean)


def _pool_kernel(u_ref, w_ref, sc_ref, o_ref, p_ref, a2_ref, a4_ref, a8_ref, *, seq):
    u = u_ref[0]
    width = u.shape[1]
    pad = PAD_ROWS
    span = seq + pad
    zero_pad = jnp.zeros((pad, width), F32)
    zero_edge = jnp.zeros((SUBLANES, width), F32)
    p_ref[0:pad] = zero_pad
    p_ref[pad + seq:2 * pad + seq] = zero_pad
    p_ref[pad:pad + seq] = u
    for ref in (a2_ref, a4_ref, a8_ref):
        ref[0:SUBLANES] = zero_edge
        ref[SUBLANES + span:2 * SUBLANES + span] = zero_edge
    a2_ref[SUBLANES:SUBLANES + span] = p_ref[7:7 + span] + p_ref[8:8 + span]
    a4_ref[SUBLANES:SUBLANES + span] = a2_ref[7:7 + span] + a2_ref[9:9 + span]
    a8_ref[SUBLANES:SUBLANES + span] = a4_ref[6:6 + span] + a4_ref[10:10 + span]
    s16 = a8_ref[pad - 4:pad - 4 + seq] + a8_ref[pad + 4:pad + 4 + seq]
    s2 = a2_ref[pad:pad + seq]
    s4 = a4_ref[pad:pad + seq]
    s8 = a8_ref[pad:pad + seq]
    lane = lax.broadcasted_iota(jnp.int32, (seq, width), 1)
    t = lax.broadcasted_iota(jnp.int32, (seq, width), 0)
    g0, g1, g2 = lane < POOL_GROUP, lane < 2 * POOL_GROUP, lane < 3 * POOL_GROUP
    half = jnp.where(g0, POOL_WINDOWS[0] // 2,
                     jnp.where(g1, POOL_WINDOWS[1] // 2,
                               jnp.where(g2, POOL_WINDOWS[2] // 2, POOL_WINDOWS[3] // 2)))
    count = (jnp.minimum(t + half, seq) - jnp.maximum(t - half, 0)).astype(F32)
    wsum = jnp.where(g0, s2, jnp.where(g1, s4, jnp.where(g2, s8, s16)))
    pooled = wsum / count - u
    o_ref[0] = (_dot(pooled.astype(BF16), w_ref[...]) * sc_ref[...]).astype(BF16)


def _pool_call(u, w_bd, scale):
    n_seq, lq, width = u.shape
    buf = pltpu.VMEM((lq + 2 * PAD_ROWS, width), F32)
    return pl.pallas_call(
        functools.partial(_pool_kernel, seq=lq),
        grid=(n_seq,),
        in_specs=[
            pl.BlockSpec((1, lq, width), lambda s: (s, 0, 0)),
            _resident((width, width)),
            _resident((1, width)),
        ],
        out_specs=pl.BlockSpec((1, lq, width), lambda s: (s, 0, 0)),
        out_shape=jax.ShapeDtypeStruct((n_seq, lq, width), BF16),
        scratch_shapes=[buf, buf, buf, buf],
        compiler_params=_params(1),
        name="pool",
    )(u, w_bd, scale.reshape(1, width))


def _dft_tables(b):
    n = 2 * b
    f = np.arange(b, dtype=np.int64)[:, None]
    s = np.arange(b, dtype=np.int64)[None, :]

    def packed(expo):
        ang = 2.0 * np.pi * ((f * expo) % n) / n
        w = np.concatenate([np.cos(ang), -np.sin(ang)], axis=0)
        w[b] = np.where((expo[0] % 2) == 0, 1.0, -1.0)
        return w

    fwd = packed(s)
    rev = packed(b - 1 - s)
    t = np.arange(b, dtype=np.int64)[:, None]
    fr = np.arange(b, dtype=np.int64)[None, :]
    ang = 2.0 * np.pi * ((fr * t) % n) / n
    inv = np.concatenate([2.0 * np.cos(ang), -2.0 * np.sin(ang)], axis=1) / n
    inv[:, 0] = 1.0 / n
    inv[:, b] = np.where((t[:, 0] % 2) == 0, 1.0, -1.0) / n
    return fwd.astype(np.float32), rev.astype(np.float32), inv.astype(np.float32)


def _hyena_tables(seq):
    t = np.linspace(0.0, 1.0, seq)[:, None]
    bands = (HYENA_EMB_DIM - 1) // 2
    w_ang = 2.0 * np.pi * np.arange(seq) / seq
    freqs = np.linspace(1e-4, bands - 1, bands)
    ang = w_ang[:, None] * freqs[None, :]
    z = np.concatenate([t, np.cos(ang), -np.sin(ang)], axis=-1)
    max_decay = math.log(HYENA_DECAY_TARGET) / HYENA_FAST_DECAY_PCT
    min_decay = math.log(HYENA_DECAY_TARGET) / HYENA_SLOW_DECAY_PCT
    deltas = np.linspace(min_decay, max_decay, HYENA_WIDTH)
    decay = np.exp(-t * np.abs(deltas))
    zpad = np.zeros((seq, LANES))
    zpad[:, :HYENA_EMB_DIM] = z
    z_shift = np.zeros_like(zpad)
    z_shift[:-1] = zpad[1:]
    decay_shift = np.zeros_like(decay)
    decay_shift[:-1] = decay[1:]
    return tuple(a.astype(np.float32) for a in (zpad, z_shift, decay, decay_shift))


def _dot_table(w_hi, w_lo, x):
    x_hi, x_lo = _split_bf16(x)
    return _dot(w_hi, x_hi) + _dot(w_hi, x_lo) + _dot(w_lo, x_hi)


def _filter_kernel(z_ref, zs_ref, dec_ref, decs_ref, w1_ref, b1_ref, w2_ref, b2_ref, w3_ref, sf_ref,
                   fh_ref, fl_ref, rh_ref, rl_ref, o_ref, *, blk, nblk):
    sf = sf_ref[...]

    def mlp(z):
        h1 = jnp.sin(sf[0:1] * (_dot3(z, w1_ref[...]) + b1_ref[...]))
        return jnp.sin(sf[1:2] * (_dot3(h1, w2_ref[...]) + b2_ref[...]))

    w3 = w3_ref[...]
    fwd = _dot3(mlp(z_ref[...]), w3[:, :HYENA_WIDTH]) * dec_ref[...]
    bwd = _dot3(mlp(zs_ref[...]), w3[:, HYENA_WIDTH:]) * decs_ref[...]
    norm = (jnp.sum(jnp.abs(fwd), axis=0, keepdims=True)
            + jnp.sum(jnp.abs(bwd), axis=0, keepdims=True))
    fwd = fwd / norm
    bwd = bwd / norm
    fh, fl, rh, rl = fh_ref[...], fl_ref[...], rh_ref[...], rl_ref[...]
    spec = {}
    for d in range(nblk):
        spec[d] = _dot_table(fh, fl, fwd[d * blk:(d + 1) * blk])
        spec[-(d + 1)] = _dot_table(rh, rl, bwd[d * blk:(d + 1) * blk])
    row = lax.broadcasted_iota(jnp.int32, (2 * blk, HYENA_WIDTH), 0)
    sign = (1 - 2 * (row & 1)).astype(F32)
    for d in range(-(nblk - 1), nblk):
        o_ref[0, d + nblk - 1] = spec[d] + sign * spec[d - 1]


def _filter_call(tables, dft, w1, b1, w2, b2, w3, sf, *, seq, blk):
    nblk = seq // blk
    z, zs, dec, decs = tables
    fh, fl, rh, rl = dft
    fo = HYENA_FILTER_ORDER
    return pl.pallas_call(
        functools.partial(_filter_kernel, blk=blk, nblk=nblk),
        grid=(HYENA_ORDER,),
        in_specs=[
            _resident((seq, LANES)), _resident((seq, LANES)),
            _resident((seq, HYENA_WIDTH)), _resident((seq, HYENA_WIDTH)),
            _resident((LANES, fo)), _resident((1, fo)),
            _resident((fo, fo)), _resident((1, fo)),
            pl.BlockSpec((fo, 2 * HYENA_WIDTH), lambda o: (0, o)),
            _resident((2, fo)),
            _resident((2 * blk, blk)), _resident((2 * blk, blk)),
            _resident((2 * blk, blk)), _resident((2 * blk, blk)),
        ],
        out_specs=pl.BlockSpec((1, 2 * nblk - 1, 2 * blk, HYENA_WIDTH), lambda o: (o, 0, 0, 0)),
        out_shape=jax.ShapeDtypeStruct((HYENA_ORDER, 2 * nblk - 1, 2 * blk, HYENA_WIDTH), F32),
        compiler_params=_params(1),
        name="hyena_filter",
    )(z, zs, dec, decs, w1, b1.reshape(1, fo), w2, b2.reshape(1, fo), w3, sf, fh, fl, rh, rl)


def _hyena_kernel(p_ref, cw_ref, cb_ref, hb_ref, wf_ref, wi_ref, ch_ref, o_ref, pad_ref, *, seq, blk):
    nblk = seq // blk
    width = p_ref.shape[2]
    zero_edge = jnp.zeros((SUBLANES, width), F32)
    pad_ref[0:SUBLANES] = zero_edge
    pad_ref[SUBLANES + seq:2 * SUBLANES + seq] = zero_edge
    pad_ref[SUBLANES:SUBLANES + seq] = p_ref[0]
    cw = cw_ref[...]
    zc = (pad_ref[SUBLANES - 1:SUBLANES - 1 + seq] * cw[0:1]
          + pad_ref[SUBLANES:SUBLANES + seq] * cw[1:2]
          + pad_ref[SUBLANES + 1:SUBLANES + 1 + seq] * cw[2:3] + cb_ref[...])
    y = zc[:, :HYENA_WIDTH]
    wf = wf_ref[...]
    wi = wi_ref[...]
    hb = hb_ref[...]
    first = lax.broadcasted_iota(jnp.int32, (blk, HYENA_WIDTH), 0) == 0
    for o in range(HYENA_ORDER):
        gate = zc[:, (o + 1) * HYENA_WIDTH:(o + 2) * HYENA_WIDTH]
        spec = [_dot(wf, y[j * blk:(j + 1) * blk].astype(BF16)) for j in range(nblk)]
        outs = []
        for i in range(nblk):
            acc_r = jnp.zeros((blk, HYENA_WIDTH), F32)
            acc_i = jnp.zeros((blk, HYENA_WIDTH), F32)
            for j in range(nblk):
                cm = ch_ref[o, i - j + nblk - 1]
                ur, ui = spec[j][:blk], spec[j][blk:]
                cr, ci = cm[:blk], cm[blk:]
                ii = ui * ci
                acc_r = acc_r + ur * cr - jnp.where(first, 0.0, ii)
                acc_i = acc_i + jnp.where(first, ii, ur * ci + ui * cr)
            prod = jnp.concatenate([acc_r, acc_i], axis=0).astype(BF16)
            outs.append(_dot(wi, prod))
        conv = outs[0] if nblk == 1 else jnp.concatenate(outs, axis=0)
        y = gate * (conv + hb[o:o + 1] * y)
    o_ref[0] = y.astype(BF16)


def _hyena_call(p, conv_w, conv_b, hy_bias, wf, wi, chat, *, blk):
    n_seq, lq, width = p.shape
    nblk = lq // blk
    return pl.pallas_call(
        functools.partial(_hyena_kernel, seq=lq, blk=blk),
        grid=(n_seq,),
        in_specs=[
            pl.BlockSpec((1, lq, width), lambda s: (s, 0, 0)),
            _resident((3, width)),
            _resident((1, width)),
            _resident((HYENA_ORDER, HYENA_WIDTH)),
            _resident((2 * blk, blk)),
            _resident((blk, 2 * blk)),
            _resident((HYENA_ORDER, 2 * nblk - 1, 2 * blk, HYENA_WIDTH)),
        ],
        out_specs=pl.BlockSpec((1, lq, HYENA_WIDTH), lambda s: (s, 0, 0)),
        out_shape=jax.ShapeDtypeStruct((n_seq, lq, HYENA_WIDTH), BF16),
        scratch_shapes=[pltpu.VMEM((lq + 2 * SUBLANES, width), F32)],
        compiler_params=_params(1),
        name="hyena",
    )(p, conv_w, conv_b.reshape(1, width), hy_bias, wf, wi, chat)


def _qk(q, k):
    return lax.dot_general(q, k, (((1,), (1,)), ((), ())), preferred_element_type=F32)


def _attn_kernel(*refs, with_latent):
    if with_latent:
        q_ref, k_ref, v_ref, kc_ref, vc_ref, o_ref = refs
    else:
        q_ref, kc_ref, vc_ref, o_ref = refs
    q = q_ref[0]
    tq = q.shape[0]
    outs = []
    for kh in range(N_KV_HEADS):
        cols = slice(kh * HEAD_DIM, (kh + 1) * HEAD_DIM)
        heads = range(kh * GQA_GROUP, (kh + 1) * GQA_GROUP)
        qs = jnp.concatenate([q[:, h * HEAD_DIM:(h + 1) * HEAD_DIM] for h in heads], axis=0)
        s_c = _qk(qs, kc_ref[0][:, cols])
        m = jnp.max(s_c, axis=-1, keepdims=True)
        if with_latent:
            s_l = _qk(qs, k_ref[0][:, cols])
            m = jnp.maximum(m, jnp.max(s_l, axis=-1, keepdims=True))
        p_c = jnp.exp(s_c - m)
        den = jnp.sum(p_c, axis=-1, keepdims=True)
        o = _dot(p_c.astype(BF16), vc_ref[0][:, cols])
        if with_latent:
            p_l = jnp.exp(s_l - m)
            den = den + jnp.sum(p_l, axis=-1, keepdims=True)
            o = o + _dot(p_l.astype(BF16), v_ref[0][:, cols])
        o = o / den
        outs.extend(o[g * tq:(g + 1) * tq] for g in range(GQA_GROUP))
    o_ref[0] = jnp.concatenate(outs, axis=1).astype(BF16)


def _attn_call(q, kc, vc, k=None, v=None):
    n_seq, lq, _ = q.shape
    tq = min(Q_TILE, lq)
    ctx_len = kc.shape[1]
    with_latent = k is not None
    q_spec = pl.BlockSpec((1, tq, ATTN_WIDTH), lambda s, i: (s, i, 0))
    kv_spec = lambda n: pl.BlockSpec((1, n, KV_WIDTH), lambda s, i: (s, 0, 0))
    if with_latent:
        in_specs = [q_spec, kv_spec(lq), kv_spec(lq), kv_spec(ctx_len), kv_spec(ctx_len)]
        args = (q, k, v, kc, vc)
    else:
        in_specs = [q_spec, kv_spec(ctx_len), kv_spec(ctx_len)]
        args = (q, kc, vc)
    return pl.pallas_call(
        functools.partial(_attn_kernel, with_latent=with_latent),
        grid=(n_seq, lq // tq),
        in_specs=in_specs,
        out_specs=q_spec,
        out_shape=jax.ShapeDtypeStruct((n_seq, lq, ATTN_WIDTH), BF16),
        compiler_params=_params(2),
        name="attn" if with_latent else "attn_ctx",
    )(*args)


def _outproj_kernel(h_ref, mod_ref, a_ref, b_ref, c_ref, w_ref, o_ref):
    m = mod_ref[0]
    wa = a_ref.shape[2]
    wb = b_ref.shape[2]
    mix = (_dot(a_ref[0], w_ref[0:wa, :]) + _dot(b_ref[0], w_ref[wa:wa + wb, :])
           + _dot(c_ref[0], w_ref[wa + wb:, :]))
    o_ref[0] = h_ref[0] + m[5:6] * mix


def _outproj_call(h, mods, mix_a, mix_b, mix_c, w_out, *, mod_row):
    n_seq, lq, d = h.shape
    tm = min(TOKEN_TILE, lq)
    tok = lambda w: pl.BlockSpec((1, tm, w), lambda s, i: (s, i, 0))
    return pl.pallas_call(
        _outproj_kernel,
        grid=(n_seq, lq // tm),
        in_specs=[
            tok(d),
            pl.BlockSpec((1, N_MOD, d), lambda s, i: (mod_row(s), 0, 0)),
            tok(mix_a.shape[2]), tok(mix_b.shape[2]), tok(mix_c.shape[2]),
            _resident(w_out.shape),
        ],
        out_specs=tok(d),
        out_shape=jax.ShapeDtypeStruct((n_seq, lq, d), F32),
        compiler_params=_params(2),
        name="outproj",
    )(h, mods, mix_a, mix_b, mix_c, w_out)


def _rope_tables(seq):
    rows = seq // GRID_W
    row = np.repeat(np.arange(rows), GRID_W).astype(np.float64)
    col = np.tile(np.arange(GRID_W), rows).astype(np.float64)
    inv_freq = 1.0 / (ROPE_THETA ** (np.arange(0, AXIS_DIM, 2, dtype=np.float64) / AXIS_DIM))
    ang_r = row[:, None] * inv_freq
    ang_c = col[:, None] * inv_freq
    ang = np.concatenate([ang_r, ang_r, ang_c, ang_c], axis=-1)
    cos, sin = np.cos(ang), np.sin(ang)
    upper = (np.arange(HEAD_DIM) % AXIS_DIM) >= AXIS_DIM // 2
    sin_a = np.where(upper, sin, 0.0)
    sin_b = np.where(upper, 0.0, -sin)
    rep = LANES // HEAD_DIM
    return tuple(jnp.asarray(np.tile(a, (1, rep)).astype(np.float32)) for a in (cos, sin_a, sin_b))


def _identity_tables(seq):
    one = jnp.ones((seq, LANES), F32)
    zero = jnp.zeros((seq, LANES), F32)
    return one, zero, zero


def _dft_operands(blk):
    fwd, rev, inv = (jnp.asarray(a) for a in _dft_tables(blk))
    fh, fl = _split_bf16(fwd)
    rh, rl = _split_bf16(rev)
    return (fh, fl, rh, rl), fh, inv.astype(BF16)


def kernel(x, c, ctx, c_ctx, norm_g, w_mod, b_mod, ffn_w_gate, ffn_w_up, ffn_w_down, w_in, w_out, pool_w, pool_scale, hyena_conv_w, hyena_conv_b, hyena_f_w1, hyena_f_b1, hyena_f_w2, hyena_f_b2, hyena_f_w3, hyena_sin_freq, hyena_bias, q_norm_g, k_norm_g):
    batch, seq, d = x.shape
    ctx_len = ctx.shape[1]
    depth = w_mod.shape[0]

    rows = -(-(batch + 1) // SUBLANES) * SUBLANES
    cvec = jnp.zeros((rows, d), F32).at[:batch].set(c).at[batch].set(c_ctx)
    mods = _mods_call(cvec, w_mod, b_mod).reshape(depth * rows, N_MOD, d)

    blk_x = min(DFT_BLOCK, seq)
    blk_c = min(DFT_BLOCK, ctx_len)
    dft_x = _dft_operands(blk_x)
    dft_c = _dft_operands(blk_c)
    hy_tab_x = tuple(jnp.asarray(a) for a in _hyena_tables(seq))
    hy_tab_c = tuple(jnp.asarray(a) for a in _hyena_tables(ctx_len))
    rope_x = _rope_tables(seq)
    rope_c = _identity_tables(ctx_len)
    seg = np.arange(LANES) // HEAD_DIM
    seg_mean = jnp.asarray((seg[:, None] == seg[None, :]).astype(np.float32) / HEAD_DIM).astype(BF16)
    n_win = len(POOL_WINDOWS)

    h, hc = x, ctx
    for l in range(depth):
        last = l == depth - 1
        row_x = lambda s, l=l: l * rows + s
        row_c = lambda s, l=l: l * rows + batch
        f1 = tuple(w[l, 0].astype(BF16) for w in (ffn_w_gate, ffn_w_up, ffn_w_down))
        f2 = tuple(w[l, 1].astype(BF16) for w in (ffn_w_gate, ffn_w_up, ffn_w_down))
        w_in_l = w_in[l].astype(BF16)
        w_out_l = w_out[l].astype(BF16)
        w_pool = jnp.zeros((n_win, POOL_GROUP, n_win, POOL_GROUP), F32)
        for gi in range(n_win):
            w_pool = w_pool.at[gi, :, gi, :].set(pool_w[l, gi])
        w_pool = w_pool.reshape(POOL_WIDTH, POOL_WIDTH).astype(BF16)
        qg = jnp.tile(q_norm_g[l], LANES // HEAD_DIM).reshape(1, LANES)
        kg = jnp.tile(k_norm_g[l], LANES // HEAD_DIM).reshape(1, LANES)
        f_w1 = jnp.zeros((LANES, HYENA_FILTER_ORDER), F32).at[:HYENA_EMB_DIM].set(hyena_f_w1[l])
        filt_args = (f_w1, hyena_f_b1[l], hyena_f_w2[l], hyena_f_b2[l], hyena_f_w3[l], hyena_sin_freq[l])

        h = _ffn_call(h, mods, norm_g[l, 0], *f1, mod_row=row_x, mod_off=0)
        hc = _ffn_call(hc, mods, norm_g[l, 0], *f1, mod_row=row_c, mod_off=0)

        pool_x, hy_x, q_x, k_x, v_x = _inproj_call(
            h, mods, norm_g[l, 1], w_in_l, rope_x, qg, kg, seg_mean, mod_row=row_x)
        pool_c, hy_c, q_c, k_c, v_c = _inproj_call(
            hc, mods, norm_g[l, 1], w_in_l, rope_c, qg, kg, seg_mean, mod_row=row_c)

        chat_x = _filter_call(hy_tab_x, dft_x[0], *filt_args, seq=seq, blk=blk_x)
        mix_x = (
            _pool_call(pool_x, w_pool, pool_scale[l]),
            _hyena_call(hy_x, hyena_conv_w[l], hyena_conv_b[l], hyena_bias[l], dft_x[1], dft_x[2],
                        chat_x, blk=blk_x),
            _attn_call(q_x, k_c, v_c, k_x, v_x),
        )
        h = _outproj_call(h, mods, *mix_x, w_out_l, mod_row=row_x)

        if not last:
            chat_c = _filter_call(hy_tab_c, dft_c[0], *filt_args, seq=ctx_len, blk=blk_c)
            mix_c = (
                _pool_call(pool_c, w_pool, pool_scale[l]),
                _hyena_call(hy_c, hyena_conv_w[l], hyena_conv_b[l], hyena_bias[l], dft_c[1], dft_c[2],
                            chat_c, blk=blk_c),
                _attn_call(q_c, k_c, v_c),
            )
            hc = _outproj_call(hc, mods, *mix_c, w_out_l, mod_row=row_c)
            hc = _ffn_call(hc, mods, norm_g[l, 2], *f2, mod_row=row_c, mod_off=6)

        h = _ffn_call(h, mods, norm_g[l, 2], *f2, mod_row=row_x, mod_off=6)
    return h
```

```python
import functools
import math

import jax
import jax.numpy as jnp
import numpy as np
from jax import lax
from jax.experimental import pallas as pl
from jax.experimental.pallas import tpu as pltpu

F32 = jnp.float32
BF16 = jnp.bfloat16

GRID_W = 64
EPS = 1e-6
N_MOD = 9
POOL_WIDTH = 256
POOL_WINDOWS = (2, 4, 8, 16)
POOL_GROUP = POOL_WIDTH // len(POOL_WINDOWS)
HYENA_WIDTH = 256
HYENA_ORDER = 2
HYENA_EMB_DIM = 33
HYENA_FILTER_ORDER = 64
HYENA_DECAY_TARGET = 1e-2
HYENA_FAST_DECAY_PCT = 0.3
HYENA_SLOW_DECAY_PCT = 1.5
HEAD_DIM = 64
N_Q_HEADS = 8
N_KV_HEADS = 2
GQA_GROUP = N_Q_HEADS // N_KV_HEADS
ATTN_WIDTH = N_Q_HEADS * HEAD_DIM
KV_WIDTH = N_KV_HEADS * HEAD_DIM
AXIS_DIM = HEAD_DIM // 2
ROPE_THETA = 10000.0
ATTN_SCALE = HEAD_DIM ** -0.5
Q_PRESCALE = ATTN_SCALE * math.log2(math.e)
HYENA_OFF = POOL_WIDTH
Q_OFF = HYENA_OFF + (HYENA_ORDER + 1) * HYENA_WIDTH
K_OFF = Q_OFF + ATTN_WIDTH
V_OFF = K_OFF + KV_WIDTH
IN_WIDTH = V_OFF + KV_WIDTH

LANES = 128
SUBLANES = 8
VMEM_LIMIT_BYTES = 56 * 1024 * 1024

TOKEN_TILE = 512
Q_TILE = 256
Q_SUB = 128
DFT_BLOCK = 512
MOD_COL_TILE = 1152
PAD_ROWS = 16


def _params(n_axes):
    return pltpu.CompilerParams(
        dimension_semantics=("arbitrary",) * n_axes, vmem_limit_bytes=VMEM_LIMIT_BYTES)


def _resident(shape):
    zeros = (0,) * len(shape)
    return pl.BlockSpec(shape, lambda *_: zeros, pipeline_mode=pl.Buffered(1))


def _split_bf16(a):
    hi = a.astype(BF16)
    lo = (a - hi.astype(F32)).astype(BF16)
    return hi, lo


def _dot(a, b):
    return jnp.dot(a, b, preferred_element_type=F32)


def _dot3(a, b):
    a_hi, a_lo = _split_bf16(a)
    b_hi, b_lo = _split_bf16(b)
    return _dot(a_hi, b_hi) + _dot(a_lo, b_hi) + _dot(a_hi, b_lo)


def _norm_mod(x, g, shift, scale):
    ms = jnp.mean(x * x, axis=-1, keepdims=True)
    y = x * lax.rsqrt(ms + EPS) * g
    return y * (1.0 + scale) + shift


def _mods_kernel(c_ref, w_ref, b_ref, o_ref):
    rows = c_ref.shape[0]
    cc = c_ref[...]
    act = cc * jax.nn.sigmoid(cc)
    a_hi, a_lo = _split_bf16(act)
    w_hi, w_lo = _split_bf16(w_ref[0])
    r = _dot(jnp.concatenate([a_hi, a_lo], axis=0), w_hi)
    o_ref[0] = r[:rows] + r[rows:] + _dot(a_hi, w_lo) + b_ref[0]


def _mods_call(cvec, w_mod, b_mod):
    depth, d, n = w_mod.shape
    rows = cvec.shape[0]
    tn = MOD_COL_TILE
    return pl.pallas_call(
        _mods_kernel,
        grid=(depth, n // tn),
        in_specs=[
            pl.BlockSpec((rows, d), lambda l, j: (0, 0)),
            pl.BlockSpec((1, d, tn), lambda l, j: (l, 0, j)),
            pl.BlockSpec((1, 1, tn), lambda l, j: (l, 0, j)),
        ],
        out_specs=pl.BlockSpec((1, rows, tn), lambda l, j: (l, 0, j)),
        out_shape=jax.ShapeDtypeStruct((depth, rows, n), F32),
        compiler_params=_params(2),
        name="mods",
    )(cvec, w_mod, b_mod.reshape(depth, 1, n))


FF_CHUNK = 1024


def _ffn_body(x, m, g, wg_ref, wu_ref, wd_ref, mod_off):
    y = _norm_mod(x, g, m[mod_off:mod_off + 1], m[mod_off + 1:mod_off + 2]).astype(BF16)
    d_ff = wg_ref.shape[1]
    acc = jnp.zeros(x.shape, F32)
    for lo in range(0, d_ff, FF_CHUNK):
        hi = min(lo + FF_CHUNK, d_ff)
        a = _dot(y, wg_ref[:, lo:hi])
        u = _dot(y, wu_ref[:, lo:hi])
        mid = (a * jax.nn.sigmoid(a) * u).astype(BF16)
        acc = acc + _dot(mid, wd_ref[lo:hi, :])
    return x + 0.5 * m[mod_off + 2:mod_off + 3] * acc


def _head_norm_rope(x, g, seg_mean, cos, sin_a, sin_b):
    sq_hi, sq_lo = _split_bf16(x * x)
    ms = _dot(sq_hi, seg_mean) + _dot(sq_lo, seg_mean)
    y = x * lax.rsqrt(ms + EPS) * g
    return (y * cos + pltpu.roll(y, AXIS_DIM // 2, 1) * sin_a
            + pltpu.roll(y, LANES - AXIS_DIM // 2, 1) * sin_b)


def _ffn_in_kernel(x_ref, mod_ref, g0_ref, wg_ref, wu_ref, wd_ref, g1_ref, w_ref,
                   cos_ref, sa_ref, sb_ref, qg_ref, kg_ref, sm_ref,
                   h_ref, pool_ref, hy_ref, q_ref, k_ref, v_ref):
    m = mod_ref[0]
    h = _ffn_body(x_ref[0], m, g0_ref[...], wg_ref, wu_ref, wd_ref, 0)
    h_ref[0] = h
    y = _norm_mod(h, g1_ref[...], m[3:4], m[4:5]).astype(BF16)
    p = _dot(y, w_ref[...])
    pool_ref[0] = p[:, :HYENA_OFF]
    hy_ref[0] = p[:, HYENA_OFF:Q_OFF]
    cos, sa, sb, sm = cos_ref[...], sa_ref[...], sb_ref[...], sm_ref[...]
    qs = []
    for j in range(ATTN_WIDTH // LANES):
        qj = p[:, Q_OFF + j * LANES:Q_OFF + (j + 1) * LANES]
        qs.append(_head_norm_rope(qj, qg_ref[...], sm, cos, sa, sb) * Q_PRESCALE)
    q_ref[0] = jnp.concatenate(qs, axis=1).astype(BF16)
    k_ref[0] = _head_norm_rope(p[:, K_OFF:V_OFF], kg_ref[...], sm, cos, sa, sb).astype(BF16)
    v_ref[0] = p[:, V_OFF:].astype(BF16)


def _ffn_in_call(h, mods, g0, ffn_w, g1, w_in, tables, qg, kg, seg_mean, *, mod_row):
    n_seq, lq, d = h.shape
    tm = min(TOKEN_TILE, lq)
    wg, wu, wd = ffn_w
    d_ff = wg.shape[1]
    cos, sa, sb = tables
    tab_spec = pl.BlockSpec((tm, LANES), lambda s, i: (i, 0))
    tok = lambda w: pl.BlockSpec((1, tm, w), lambda s, i: (s, i, 0))
    shp = lambda w, dt: jax.ShapeDtypeStruct((n_seq, lq, w), dt)
    return pl.pallas_call(
        _ffn_in_kernel,
        grid=(n_seq, lq // tm),
        in_specs=[
            tok(d),
            pl.BlockSpec((1, N_MOD, d), lambda s, i: (mod_row(s), 0, 0)),
            _resident((1, d)),
            _resident((d, d_ff)), _resident((d, d_ff)), _resident((d_ff, d)),
            _resident((1, d)),
            _resident((d, IN_WIDTH)),
            tab_spec, tab_spec, tab_spec,
            _resident((1, LANES)), _resident((1, LANES)), _resident((LANES, LANES)),
        ],
        out_specs=[tok(d), tok(POOL_WIDTH), tok(Q_OFF - HYENA_OFF), tok(ATTN_WIDTH),
                   tok(KV_WIDTH), tok(KV_WIDTH)],
        out_shape=[shp(d, F32), shp(POOL_WIDTH, F32), shp(Q_OFF - HYENA_OFF, F32),
                   shp(ATTN_WIDTH, BF16), shp(KV_WIDTH, BF16), shp(KV_WIDTH, BF16)],
        compiler_params=_params(2),
        name="ffn_in",
    )(h, mods, g0.reshape(1, d), wg, wu, wd, g1.reshape(1, d), w_in, cos, sa, sb, qg, kg, seg_mean)


def _ffn_out_kernel(h_ref, mod_ref, a_ref, b_ref, c_ref, w_ref, g_ref, wg_ref, wu_ref, wd_ref, o_ref):
    m = mod_ref[0]
    wa = a_ref.shape[2]
    wb = b_ref.shape[2]
    mix = (_dot(a_ref[0], w_ref[0:wa, :]) + _dot(b_ref[0], w_ref[wa:wa + wb, :])
           + _dot(c_ref[0], w_ref[wa + wb:, :]))
    h = h_ref[0] + m[5:6] * mix
    o_ref[0] = _ffn_body(h, m, g_ref[...], wg_ref, wu_ref, wd_ref, 6)


def _ffn_out_call(h, mods, mix, w_out, g, ffn_w, *, mod_row):
    n_seq, lq, d = h.shape
    tm = min(TOKEN_TILE, lq)
    wg, wu, wd = ffn_w
    d_ff = wg.shape[1]
    tok = lambda w: pl.BlockSpec((1, tm, w), lambda s, i: (s, i, 0))
    return pl.pallas_call(
        _ffn_out_kernel,
        grid=(n_seq, lq // tm),
        in_specs=[
            tok(d),
            pl.BlockSpec((1, N_MOD, d), lambda s, i: (mod_row(s), 0, 0)),
            tok(mix[0].shape[2]), tok(mix[1].shape[2]), tok(mix[2].shape[2]),
            _resident(w_out.shape),
            _resident((1, d)),
            _resident((d, d_ff)), _resident((d, d_ff)), _resident((d_ff, d)),
        ],
        out_specs=tok(d),
        out_shape=jax.ShapeDtypeStruct((n_seq, lq, d), F32),
        compiler_params=_params(2),
        name="ffn_out",
    )(h, mods, *mix, w_out, g.reshape(1, d), wg, wu, wd)


def _pool_kernel(u_ref, w_ref, sc_ref, o_ref, p_ref, a2_ref, a4_ref, a8_ref, *, seq):
    u = u_ref[0]
    width = u.shape[1]
    pad = PAD_ROWS
    span = seq + pad
    zero_pad = jnp.zeros((pad, width), F32)
    zero_edge = jnp.zeros((SUBLANES, width), F32)
    p_ref[0:pad] = zero_pad
    p_ref[pad + seq:2 * pad + seq] = zero_pad
    p_ref[pad:pad + seq] = u
    for ref in (a2_ref, a4_ref, a8_ref):
        ref[0:SUBLANES] = zero_edge
        ref[SUBLANES + span:2 * SUBLANES + span] = zero_edge
    a2_ref[SUBLANES:SUBLANES + span] = p_ref[7:7 + span] + p_ref[8:8 + span]
    a4_ref[SUBLANES:SUBLANES + span] = a2_ref[7:7 + span] + a2_ref[9:9 + span]
    a8_ref[SUBLANES:SUBLANES + span] = a4_ref[6:6 + span] + a4_ref[10:10 + span]
    s16 = a8_ref[pad - 4:pad - 4 + seq] + a8_ref[pad + 4:pad + 4 + seq]
    s2 = a2_ref[pad:pad + seq]
    s4 = a4_ref[pad:pad + seq]
    s8 = a8_ref[pad:pad + seq]
    lane = lax.broadcasted_iota(jnp.int32, (seq, width), 1)
    t = lax.broadcasted_iota(jnp.int32, (seq, width), 0)
    g0, g1, g2 = lane < POOL_GROUP, lane < 2 * POOL_GROUP, lane < 3 * POOL_GROUP
    half = jnp.where(g0, POOL_WINDOWS[0] // 2,
                     jnp.where(g1, POOL_WINDOWS[1] // 2,
                               jnp.where(g2, POOL_WINDOWS[2] // 2, POOL_WINDOWS[3] // 2)))
    count = (jnp.minimum(t + half, seq) - jnp.maximum(t - half, 0)).astype(F32)
    wsum = jnp.where(g0, s2, jnp.where(g1, s4, jnp.where(g2, s8, s16)))
    pooled = wsum / count - u
    o_ref[0] = (_dot(pooled.astype(BF16), w_ref[...]) * sc_ref[...]).astype(BF16)


def _pool_call(u, w_bd, scale):
    n_seq, lq, width = u.shape
    buf = pltpu.VMEM((lq + 2 * PAD_ROWS, width), F32)
    return pl.pallas_call(
        functools.partial(_pool_kernel, seq=lq),
        grid=(n_seq,),
        in_specs=[
            pl.BlockSpec((1, lq, width), lambda s: (s, 0, 0)),
            _resident((width, width)),
            _resident((1, width)),
        ],
        out_specs=pl.BlockSpec((1, lq, width), lambda s: (s, 0, 0)),
        out_shape=jax.ShapeDtypeStruct((n_seq, lq, width), BF16),
        scratch_shapes=[buf, buf, buf, buf],
        compiler_params=_params(1),
        name="pool",
    )(u, w_bd, scale.reshape(1, width))


def _dft_tables(b):
    n = 2 * b
    f = np.arange(b, dtype=np.int64)[:, None]
    s = np.arange(b, dtype=np.int64)[None, :]

    def packed(expo):
        ang = 2.0 * np.pi * ((f * expo) % n) / n
        w = np.concatenate([np.cos(ang), -np.sin(ang)], axis=0)
        w[b] = np.where((expo[0] % 2) == 0, 1.0, -1.0)
        return w

    fwd = packed(s)
    rev = packed(b - 1 - s)
    t = np.arange(b, dtype=np.int64)[:, None]
    fr = np.arange(b, dtype=np.int64)[None, :]
    ang = 2.0 * np.pi * ((fr * t) % n) / n
    inv = np.concatenate([2.0 * np.cos(ang), -2.0 * np.sin(ang)], axis=1) / n
    inv[:, 0] = 1.0 / n
    inv[:, b] = np.where((t[:, 0] % 2) == 0, 1.0, -1.0) / n
    return fwd.astype(np.float32), rev.astype(np.float32), inv.astype(np.float32)


def _hyena_tables(seq):
    t = np.linspace(0.0, 1.0, seq)[:, None]
    bands = (HYENA_EMB_DIM - 1) // 2
    w_ang = 2.0 * np.pi * np.arange(seq) / seq
    freqs = np.linspace(1e-4, bands - 1, bands)
    ang = w_ang[:, None] * freqs[None, :]
    z = np.concatenate([t, np.cos(ang), -np.sin(ang)], axis=-1)
    max_decay = math.log(HYENA_DECAY_TARGET) / HYENA_FAST_DECAY_PCT
    min_decay = math.log(HYENA_DECAY_TARGET) / HYENA_SLOW_DECAY_PCT
    deltas = np.linspace(min_decay, max_decay, HYENA_WIDTH)
    decay = np.exp(-t * np.abs(deltas))
    zpad = np.zeros((seq, LANES))
    zpad[:, :HYENA_EMB_DIM] = z
    z_shift = np.zeros_like(zpad)
    z_shift[:-1] = zpad[1:]
    decay_shift = np.zeros_like(decay)
    decay_shift[:-1] = decay[1:]
    return tuple(a.astype(np.float32) for a in (zpad, z_shift, decay, decay_shift))


def _dot_table(w_hi, w_lo, x):
    x_hi, x_lo = _split_bf16(x)
    return _dot(w_hi, x_hi) + _dot(w_hi, x_lo) + _dot(w_lo, x_hi)


def _filter_kernel(z_ref, zs_ref, dec_ref, decs_ref, w1_ref, b1_ref, w2_ref, b2_ref, w3_ref, sf_ref,
                   fh_ref, fl_ref, rh_ref, rl_ref, o_ref, *, blk, nblk):
    sf = sf_ref[...]

    def mlp(z):
        h1 = jnp.sin(sf[0:1] * (_dot3(z, w1_ref[...]) + b1_ref[...]))
        return jnp.sin(sf[1:2] * (_dot3(h1, w2_ref[...]) + b2_ref[...]))

    w3 = w3_ref[...]
    fwd = _dot3(mlp(z_ref[...]), w3[:, :HYENA_WIDTH]) * dec_ref[...]
    bwd = _dot3(mlp(zs_ref[...]), w3[:, HYENA_WIDTH:]) * decs_ref[...]
    norm = (jnp.sum(jnp.abs(fwd), axis=0, keepdims=True)
            + jnp.sum(jnp.abs(bwd), axis=0, keepdims=True))
    fwd = fwd / norm
    bwd = bwd / norm
    fh, fl, rh, rl = fh_ref[...], fl_ref[...], rh_ref[...], rl_ref[...]
    spec = {}
    for d in range(nblk):
        spec[d] = _dot_table(fh, fl, fwd[d * blk:(d + 1) * blk])
        spec[-(d + 1)] = _dot_table(rh, rl, bwd[d * blk:(d + 1) * blk])
    row = lax.broadcasted_iota(jnp.int32, (2 * blk, HYENA_WIDTH), 0)
    sign = (1 - 2 * (row & 1)).astype(F32)
    for d in range(-(nblk - 1), nblk):
        o_ref[0, d + nblk - 1] = spec[d] + sign * spec[d - 1]


def _filter_call(tables, dft, w1, b1, w2, b2, w3, sf, *, seq, blk):
    nblk = seq // blk
    z, zs, dec, decs = tables
    fh, fl, rh, rl = dft
    fo = HYENA_FILTER_ORDER
    return pl.pallas_call(
        functools.partial(_filter_kernel, blk=blk, nblk=nblk),
        grid=(HYENA_ORDER,),
        in_specs=[
            _resident((seq, LANES)), _resident((seq, LANES)),
            _resident((seq, HYENA_WIDTH)), _resident((seq, HYENA_WIDTH)),
            _resident((LANES, fo)), _resident((1, fo)),
            _resident((fo, fo)), _resident((1, fo)),
            pl.BlockSpec((fo, 2 * HYENA_WIDTH), lambda o: (0, o)),
            _resident((2, fo)),
            _resident((2 * blk, blk)), _resident((2 * blk, blk)),
            _resident((2 * blk, blk)), _resident((2 * blk, blk)),
        ],
        out_specs=pl.BlockSpec((1, 2 * nblk - 1, 2 * blk, HYENA_WIDTH), lambda o: (o, 0, 0, 0)),
        out_shape=jax.ShapeDtypeStruct((HYENA_ORDER, 2 * nblk - 1, 2 * blk, HYENA_WIDTH), F32),
        compiler_params=_params(1),
        name="hyena_filter",
    )(z, zs, dec, decs, w1, b1.reshape(1, fo), w2, b2.reshape(1, fo), w3, sf, fh, fl, rh, rl)


def _hyena_kernel(p_ref, cw_ref, cb_ref, hb_ref, wf_ref, wi_ref, ch_ref, o_ref, pad_ref, *, seq, blk):
    nblk = seq // blk
    width = p_ref.shape[2]
    zero_edge = jnp.zeros((SUBLANES, width), F32)
    pad_ref[0:SUBLANES] = zero_edge
    pad_ref[SUBLANES + seq:2 * SUBLANES + seq] = zero_edge
    pad_ref[SUBLANES:SUBLANES + seq] = p_ref[0]
    cw = cw_ref[...]
    zc = (pad_ref[SUBLANES - 1:SUBLANES - 1 + seq] * cw[0:1]
          + pad_ref[SUBLANES:SUBLANES + seq] * cw[1:2]
          + pad_ref[SUBLANES + 1:SUBLANES + 1 + seq] * cw[2:3] + cb_ref[...])
    y = zc[:, :HYENA_WIDTH]
    wf = wf_ref[...]
    wi = wi_ref[...]
    hb = hb_ref[...]
    first = lax.broadcasted_iota(jnp.int32, (blk, HYENA_WIDTH), 0) == 0
    for o in range(HYENA_ORDER):
        gate = zc[:, (o + 1) * HYENA_WIDTH:(o + 2) * HYENA_WIDTH]
        spec = [_dot(wf, y[j * blk:(j + 1) * blk].astype(BF16)) for j in range(nblk)]
        outs = []
        for i in range(nblk):
            acc_r = jnp.zeros((blk, HYENA_WIDTH), F32)
            acc_i = jnp.zeros((blk, HYENA_WIDTH), F32)
            for j in range(nblk):
                cm = ch_ref[o, i - j + nblk - 1]
                ur, ui = spec[j][:blk], spec[j][blk:]
                cr, ci = cm[:blk], cm[blk:]
                ii = ui * ci
                acc_r = acc_r + ur * cr - jnp.where(first, 0.0, ii)
                acc_i = acc_i + jnp.where(first, ii, ur * ci + ui * cr)
            prod = jnp.concatenate([acc_r, acc_i], axis=0).astype(BF16)
            outs.append(_dot(wi, prod))
        conv = outs[0] if nblk == 1 else jnp.concatenate(outs, axis=0)
        y = gate * (conv + hb[o:o + 1] * y)
    o_ref[0] = y.astype(BF16)


def _hyena_call(p, conv_w, conv_b, hy_bias, wf, wi, chat, *, blk):
    n_seq, lq, width = p.shape
    nblk = lq // blk
    return pl.pallas_call(
        functools.partial(_hyena_kernel, seq=lq, blk=blk),
        grid=(n_seq,),
        in_specs=[
            pl.BlockSpec((1, lq, width), lambda s: (s, 0, 0)),
            _resident((3, width)),
            _resident((1, width)),
            _resident((HYENA_ORDER, HYENA_WIDTH)),
            _resident((2 * blk, blk)),
            _resident((blk, 2 * blk)),
            _resident((HYENA_ORDER, 2 * nblk - 1, 2 * blk, HYENA_WIDTH)),
        ],
        out_specs=pl.BlockSpec((1, lq, HYENA_WIDTH), lambda s: (s, 0, 0)),
        out_shape=jax.ShapeDtypeStruct((n_seq, lq, HYENA_WIDTH), BF16),
        scratch_shapes=[pltpu.VMEM((lq + 2 * SUBLANES, width), F32)],
        compiler_params=_params(1),
        name="hyena",
    )(p, conv_w, conv_b.reshape(1, width), hy_bias, wf, wi, chat)


def _qk(q, k):
    return lax.dot_general(q, k, (((1,), (1,)), ((), ())), preferred_element_type=F32)


def _attn_kernel(*refs, with_latent):
    if with_latent:
        q_ref, k_ref, v_ref, kc_ref, vc_ref, o_ref = refs
    else:
        q_ref, kc_ref, vc_ref, o_ref = refs
    q = q_ref[0]
    tq = q.shape[0]
    sub = min(Q_SUB, tq)
    pieces = [[None] * N_Q_HEADS for _ in range(tq // sub)]
    for kh in range(N_KV_HEADS):
        cols = slice(kh * HEAD_DIM, (kh + 1) * HEAD_DIM)
        k_c, v_c = kc_ref[0][:, cols], vc_ref[0][:, cols]
        if with_latent:
            k_l, v_l = k_ref[0][:, cols], v_ref[0][:, cols]
        for r in range(tq // sub):
            rows = slice(r * sub, (r + 1) * sub)
            qs = jnp.concatenate(
                [q[rows, (kh * GQA_GROUP + g) * HEAD_DIM:(kh * GQA_GROUP + g + 1) * HEAD_DIM]
                 for g in range(GQA_GROUP)], axis=0)
            s_c = _qk(qs, k_c)
            m = jnp.max(s_c, axis=-1, keepdims=True)
            if with_latent:
                s_l = _qk(qs, k_l)
                m = jnp.maximum(m, jnp.max(s_l, axis=-1, keepdims=True))
            p_c = jnp.exp2(s_c - m)
            den = jnp.sum(p_c, axis=-1, keepdims=True)
            o = _dot(p_c.astype(BF16), v_c)
            if with_latent:
                p_l = jnp.exp2(s_l - m)
                den = den + jnp.sum(p_l, axis=-1, keepdims=True)
                o = o + _dot(p_l.astype(BF16), v_l)
            o = o / den
            for g in range(GQA_GROUP):
                pieces[r][kh * GQA_GROUP + g] = o[g * sub:(g + 1) * sub]
    o_ref[0] = jnp.concatenate(
        [jnp.concatenate(row, axis=1) for row in pieces], axis=0).astype(BF16)


def _attn_call(q, kc, vc, k=None, v=None):
    n_seq, lq, _ = q.shape
    tq = min(Q_TILE, lq)
    ctx_len = kc.shape[1]
    with_latent = k is not None
    q_spec = pl.BlockSpec((1, tq, ATTN_WIDTH), lambda s, i: (s, i, 0))
    kv_spec = lambda n: pl.BlockSpec((1, n, KV_WIDTH), lambda s, i: (s, 0, 0))
    if with_latent:
        in_specs = [q_spec, kv_spec(lq), kv_spec(lq), kv_spec(ctx_len), kv_spec(ctx_len)]
        args = (q, k, v, kc, vc)
    else:
        in_specs = [q_spec, kv_spec(ctx_len), kv_spec(ctx_len)]
        args = (q, kc, vc)
    return pl.pallas_call(
        functools.partial(_attn_kernel, with_latent=with_latent),
        grid=(n_seq, lq // tq),
        in_specs=in_specs,
        out_specs=q_spec,
        out_shape=jax.ShapeDtypeStruct((n_seq, lq, ATTN_WIDTH), BF16),
        compiler_params=_params(2),
        name="attn" if with_latent else "attn_ctx",
    )(*args)


def _rope_tables(seq):
    rows = seq // GRID_W
    row = np.repeat(np.arange(rows), GRID_W).astype(np.float64)
    col = np.tile(np.arange(GRID_W), rows).astype(np.float64)
    inv_freq = 1.0 / (ROPE_THETA ** (np.arange(0, AXIS_DIM, 2, dtype=np.float64) / AXIS_DIM))
    ang_r = row[:, None] * inv_freq
    ang_c = col[:, None] * inv_freq
    ang = np.concatenate([ang_r, ang_r, ang_c, ang_c], axis=-1)
    cos, sin = np.cos(ang), np.sin(ang)
    upper = (np.arange(HEAD_DIM) % AXIS_DIM) >= AXIS_DIM // 2
    sin_a = np.where(upper, sin, 0.0)
    sin_b = np.where(upper, 0.0, -sin)
    rep = LANES // HEAD_DIM
    return tuple(jnp.asarray(np.tile(a, (1, rep)).astype(np.float32)) for a in (cos, sin_a, sin_b))


def _identity_tables(seq):
    one = jnp.ones((seq, LANES), F32)
    zero = jnp.zeros((seq, LANES), F32)
    return one, zero, zero


def _dft_operands(blk):
    fwd, rev, inv = (jnp.asarray(a) for a in _dft_tables(blk))
    fh, fl = _split_bf16(fwd)
    rh, rl = _split_bf16(rev)
    return (fh, fl, rh, rl), fh, inv.astype(BF16)


def kernel(x, c, ctx, c_ctx, norm_g, w_mod, b_mod, ffn_w_gate, ffn_w_up, ffn_w_down, w_in, w_out, pool_w, pool_scale, hyena_conv_w, hyena_conv_b, hyena_f_w1, hyena_f_b1, hyena_f_w2, hyena_f_b2, hyena_f_w3, hyena_sin_freq, hyena_bias, q_norm_g, k_norm_g):
    batch, seq, d = x.shape
    ctx_len = ctx.shape[1]
    depth = w_mod.shape[0]

    rows = -(-(batch + 1) // SUBLANES) * SUBLANES
    cvec = jnp.zeros((rows, d), F32).at[:batch].set(c).at[batch].set(c_ctx)
    mods = _mods_call(cvec, w_mod, b_mod).reshape(depth * rows, N_MOD, d)

    blk_x = min(DFT_BLOCK, seq)
    blk_c = min(DFT_BLOCK, ctx_len)
    dft_x = _dft_operands(blk_x)
    dft_c = _dft_operands(blk_c)
    hy_tab_x = tuple(jnp.asarray(a) for a in _hyena_tables(seq))
    hy_tab_c = tuple(jnp.asarray(a) for a in _hyena_tables(ctx_len))
    rope_x = _rope_tables(seq)
    rope_c = _identity_tables(ctx_len)
    seg = np.arange(LANES) // HEAD_DIM
    seg_mean = jnp.asarray((seg[:, None] == seg[None, :]).astype(np.float32) / HEAD_DIM).astype(BF16)
    n_win = len(POOL_WINDOWS)

    h, hc = x, ctx
    for l in range(depth):
        last = l == depth - 1
        row_x = lambda s, l=l: l * rows + s
        row_c = lambda s, l=l: l * rows + batch
        f1 = tuple(w[l, 0].astype(BF16) for w in (ffn_w_gate, ffn_w_up, ffn_w_down))
        f2 = tuple(w[l, 1].astype(BF16) for w in (ffn_w_gate, ffn_w_up, ffn_w_down))
        w_in_l = w_in[l].astype(BF16)
        w_out_l = w_out[l].astype(BF16)
        w_pool = jnp.zeros((n_win, POOL_GROUP, n_win, POOL_GROUP), F32)
        for gi in range(n_win):
            w_pool = w_pool.at[gi, :, gi, :].set(pool_w[l, gi])
        w_pool = w_pool.reshape(POOL_WIDTH, POOL_WIDTH).astype(BF16)
        qg = jnp.tile(q_norm_g[l], LANES // HEAD_DIM).reshape(1, LANES)
        kg = jnp.tile(k_norm_g[l], LANES // HEAD_DIM).reshape(1, LANES)
        f_w1 = jnp.zeros((LANES, HYENA_FILTER_ORDER), F32).at[:HYENA_EMB_DIM].set(hyena_f_w1[l])
        filt_args = (f_w1, hyena_f_b1[l], hyena_f_w2[l], hyena_f_b2[l], hyena_f_w3[l], hyena_sin_freq[l])

        h, pool_x, hy_x, q_x, k_x, v_x = _ffn_in_call(
            h, mods, norm_g[l, 0], f1, norm_g[l, 1], w_in_l, rope_x, qg, kg, seg_mean, mod_row=row_x)
        hc, pool_c, hy_c, q_c, k_c, v_c = _ffn_in_call(
            hc, mods, norm_g[l, 0], f1, norm_g[l, 1], w_in_l, rope_c, qg, kg, seg_mean, mod_row=row_c)

        chat_x = _filter_call(hy_tab_x, dft_x[0], *filt_args, seq=seq, blk=blk_x)
        mix_x = (
            _pool_call(pool_x, w_pool, pool_scale[l]),
            _hyena_call(hy_x, hyena_conv_w[l], hyena_conv_b[l], hyena_bias[l], dft_x[1], dft_x[2],
                        chat_x, blk=blk_x),
            _attn_call(q_x, k_c, v_c, k_x, v_x),
        )
        if not last:
            chat_c = _filter_call(hy_tab_c, dft_c[0], *filt_args, seq=ctx_len, blk=blk_c)
            mix_c = (
                _pool_call(pool_c, w_pool, pool_scale[l]),
                _hyena_call(hy_c, hyena_conv_w[l], hyena_conv_b[l], hyena_bias[l], dft_c[1], dft_c[2],
                            chat_c, blk=blk_c),
                _attn_call(q_c, k_c, v_c),
            )
            hc = _ffn_out_call(hc, mods, mix_c, w_out_l, norm_g[l, 2], f2, mod_row=row_c)
        h = _ffn_out_call(h, mods, mix_x, w_out_l, norm_g[l, 2], f2, mod_row=row_x)
    return h
```

```python
import functools
import math

import jax
import jax.numpy as jnp
import numpy as np
from jax import lax
from jax.experimental import pallas as pl
from jax.experimental.pallas import tpu as pltpu

F32 = jnp.float32
BF16 = jnp.bfloat16

GRID_W = 64
EPS = 1e-6
N_MOD = 9
POOL_WIDTH = 256
POOL_WINDOWS = (2, 4, 8, 16)
POOL_GROUP = POOL_WIDTH // len(POOL_WINDOWS)
HYENA_WIDTH = 256
HYENA_ORDER = 2
HYENA_EMB_DIM = 33
HYENA_FILTER_ORDER = 64
HYENA_DECAY_TARGET = 1e-2
HYENA_FAST_DECAY_PCT = 0.3
HYENA_SLOW_DECAY_PCT = 1.5
HEAD_DIM = 64
N_Q_HEADS = 8
N_KV_HEADS = 2
GQA_GROUP = N_Q_HEADS // N_KV_HEADS
ATTN_WIDTH = N_Q_HEADS * HEAD_DIM
KV_WIDTH = N_KV_HEADS * HEAD_DIM
AXIS_DIM = HEAD_DIM // 2
ROPE_THETA = 10000.0
ATTN_SCALE = HEAD_DIM ** -0.5
Q_PRESCALE = ATTN_SCALE * math.log2(math.e)
HYENA_OFF = POOL_WIDTH
Q_OFF = HYENA_OFF + (HYENA_ORDER + 1) * HYENA_WIDTH
K_OFF = Q_OFF + ATTN_WIDTH
V_OFF = K_OFF + KV_WIDTH
IN_WIDTH = V_OFF + KV_WIDTH

LANES = 128
SUBLANES = 8
VMEM_LIMIT_BYTES = 56 * 1024 * 1024

TOKEN_TILE = 512
Q_TILE = 512
Q_SUB = 128
DFT_BLOCK = 512
MOD_COL_TILE = 1152
PAD_ROWS = 16


def _params(n_axes, fuse_inputs=None):
    return pltpu.CompilerParams(
        dimension_semantics=("arbitrary",) * n_axes, vmem_limit_bytes=VMEM_LIMIT_BYTES,
        allow_input_fusion=fuse_inputs)


def _resident(shape):
    zeros = (0,) * len(shape)
    return pl.BlockSpec(shape, lambda *_: zeros, pipeline_mode=pl.Buffered(1))


def _split_bf16(a):
    hi = a.astype(BF16)
    lo = (a - hi.astype(F32)).astype(BF16)
    return hi, lo


def _dot(a, b):
    return jnp.dot(a, b, preferred_element_type=F32)


def _dot3(a, b):
    a_hi, a_lo = _split_bf16(a)
    b_hi, b_lo = _split_bf16(b)
    return _dot(a_hi, b_hi) + _dot(a_lo, b_hi) + _dot(a_hi, b_lo)


def _norm_mod(x, g, shift, scale):
    ms = jnp.mean(x * x, axis=-1, keepdims=True)
    y = x * lax.rsqrt(ms + EPS) * g
    return y * (1.0 + scale) + shift


def _mods_kernel(c_ref, w_ref, b_ref, o_ref):
    rows = c_ref.shape[0]
    cc = c_ref[...]
    act = cc * jax.nn.sigmoid(cc)
    a_hi, a_lo = _split_bf16(act)
    w_hi, w_lo = _split_bf16(w_ref[0])
    r = _dot(jnp.concatenate([a_hi, a_lo], axis=0), w_hi)
    o_ref[0] = r[:rows] + r[rows:] + _dot(a_hi, w_lo) + b_ref[0]


def _mods_call(cvec, w_mod, b_mod):
    depth, d, n = w_mod.shape
    rows = cvec.shape[0]
    tn = MOD_COL_TILE
    return pl.pallas_call(
        _mods_kernel,
        grid=(depth, n // tn),
        in_specs=[
            pl.BlockSpec((rows, d), lambda l, j: (0, 0)),
            pl.BlockSpec((1, d, tn), lambda l, j: (l, 0, j)),
            pl.BlockSpec((1, 1, tn), lambda l, j: (l, 0, j)),
        ],
        out_specs=pl.BlockSpec((1, rows, tn), lambda l, j: (l, 0, j)),
        out_shape=jax.ShapeDtypeStruct((depth, rows, n), F32),
        compiler_params=_params(2),
        name="mods",
    )(cvec, w_mod, b_mod.reshape(depth, 1, n))


FF_CHUNK = 1024


def _ffn_body(x, m, g, wg_ref, wu_ref, wd_ref, mod_off):
    y = _norm_mod(x, g, m[mod_off:mod_off + 1], m[mod_off + 1:mod_off + 2]).astype(BF16)
    d_ff = wg_ref.shape[1]
    acc = jnp.zeros(x.shape, F32)
    for lo in range(0, d_ff, FF_CHUNK):
        hi = min(lo + FF_CHUNK, d_ff)
        a = _dot(y, wg_ref[:, lo:hi])
        u = _dot(y, wu_ref[:, lo:hi])
        mid = (a * jax.nn.sigmoid(a) * u).astype(BF16)
        acc = acc + _dot(mid, wd_ref[lo:hi, :])
    return x + 0.5 * m[mod_off + 2:mod_off + 3] * acc


def _head_norm_rope(x, g, seg_mean, cos, sin_a, sin_b):
    sq_hi, sq_lo = _split_bf16(x * x)
    ms = _dot(sq_hi, seg_mean) + _dot(sq_lo, seg_mean)
    y = x * lax.rsqrt(ms + EPS) * g
    return (y * cos + pltpu.roll(y, AXIS_DIM // 2, 1) * sin_a
            + pltpu.roll(y, LANES - AXIS_DIM // 2, 1) * sin_b)


def _ffn_in_kernel(x_ref, mod_ref, g0_ref, wg_ref, wu_ref, wd_ref, g1_ref, w_ref,
                   cos_ref, sa_ref, sb_ref, qg_ref, kg_ref, sm_ref,
                   h_ref, pool_ref, hy_ref, q_ref, k_ref, v_ref):
    m = mod_ref[0]
    h = _ffn_body(x_ref[0], m, g0_ref[...], wg_ref, wu_ref, wd_ref, 0)
    h_ref[0] = h
    y = _norm_mod(h, g1_ref[...], m[3:4], m[4:5]).astype(BF16)
    p = _dot(y, w_ref[...])
    pool_ref[0] = p[:, :HYENA_OFF]
    hy_ref[0] = p[:, HYENA_OFF:Q_OFF]
    cos, sa, sb, sm = cos_ref[...], sa_ref[...], sb_ref[...], sm_ref[...]
    qs = []
    for j in range(ATTN_WIDTH // LANES):
        qj = p[:, Q_OFF + j * LANES:Q_OFF + (j + 1) * LANES]
        qs.append(_head_norm_rope(qj, qg_ref[...], sm, cos, sa, sb) * Q_PRESCALE)
    q_ref[0] = jnp.concatenate(qs, axis=1).astype(BF16)
    k_ref[0] = _head_norm_rope(p[:, K_OFF:V_OFF], kg_ref[...], sm, cos, sa, sb).astype(BF16)
    v_ref[0] = p[:, V_OFF:].astype(BF16)


def _ffn_in_call(h, mods, g0, ffn_w, g1, w_in, tables, qg, kg, seg_mean, *, mod_row):
    n_seq, lq, d = h.shape
    tm = min(TOKEN_TILE, lq)
    wg, wu, wd = ffn_w
    d_ff = wg.shape[1]
    cos, sa, sb = tables
    tab_spec = pl.BlockSpec((tm, LANES), lambda s, i: (i, 0))
    tok = lambda w: pl.BlockSpec((1, tm, w), lambda s, i: (s, i, 0))
    shp = lambda w, dt: jax.ShapeDtypeStruct((n_seq, lq, w), dt)
    return pl.pallas_call(
        _ffn_in_kernel,
        grid=(n_seq, lq // tm),
        in_specs=[
            tok(d),
            pl.BlockSpec((1, N_MOD, d), lambda s, i: (mod_row(s), 0, 0)),
            _resident((1, d)),
            _resident((d, d_ff)), _resident((d, d_ff)), _resident((d_ff, d)),
            _resident((1, d)),
            _resident((d, IN_WIDTH)),
            tab_spec, tab_spec, tab_spec,
            _resident((1, LANES)), _resident((1, LANES)), _resident((LANES, LANES)),
        ],
        out_specs=[tok(d), tok(POOL_WIDTH), tok(Q_OFF - HYENA_OFF), tok(ATTN_WIDTH),
                   tok(KV_WIDTH), tok(KV_WIDTH)],
        out_shape=[shp(d, F32), shp(POOL_WIDTH, F32), shp(Q_OFF - HYENA_OFF, F32),
                   shp(ATTN_WIDTH, BF16), shp(KV_WIDTH, BF16), shp(KV_WIDTH, BF16)],
        compiler_params=_params(2, [i in (3, 4, 5, 7) for i in range(14)]),
        name="ffn_in",
    )(h, mods, g0.reshape(1, d), wg, wu, wd, g1.reshape(1, d), w_in, cos, sa, sb, qg, kg, seg_mean)


def _ffn_out_kernel(h_ref, mod_ref, a_ref, b_ref, c_ref, w_ref, g_ref, wg_ref, wu_ref, wd_ref, o_ref):
    m = mod_ref[0]
    wa = a_ref.shape[2]
    wb = b_ref.shape[2]
    mix = (_dot(a_ref[0], w_ref[0:wa, :]) + _dot(b_ref[0], w_ref[wa:wa + wb, :])
           + _dot(c_ref[0], w_ref[wa + wb:, :]))
    h = h_ref[0] + m[5:6] * mix
    o_ref[0] = _ffn_body(h, m, g_ref[...], wg_ref, wu_ref, wd_ref, 6)


def _ffn_out_call(h, mods, mix, w_out, g, ffn_w, *, mod_row):
    n_seq, lq, d = h.shape
    tm = min(TOKEN_TILE, lq)
    wg, wu, wd = ffn_w
    d_ff = wg.shape[1]
    tok = lambda w: pl.BlockSpec((1, tm, w), lambda s, i: (s, i, 0))
    return pl.pallas_call(
        _ffn_out_kernel,
        grid=(n_seq, lq // tm),
        in_specs=[
            tok(d),
            pl.BlockSpec((1, N_MOD, d), lambda s, i: (mod_row(s), 0, 0)),
            tok(mix[0].shape[2]), tok(mix[1].shape[2]), tok(mix[2].shape[2]),
            _resident(w_out.shape),
            _resident((1, d)),
            _resident((d, d_ff)), _resident((d, d_ff)), _resident((d_ff, d)),
        ],
        out_specs=tok(d),
        out_shape=jax.ShapeDtypeStruct((n_seq, lq, d), F32),
        compiler_params=_params(2, [i in (5, 7, 8, 9) for i in range(10)]),
        name="ffn_out",
    )(h, mods, *mix, w_out, g.reshape(1, d), wg, wu, wd)


def _pool_kernel(u_ref, w_ref, sc_ref, o_ref, p_ref, a2_ref, a4_ref, a8_ref, *, seq):
    u = u_ref[0]
    width = u.shape[1]
    pad = PAD_ROWS
    span = seq + pad
    zero_pad = jnp.zeros((pad, width), F32)
    zero_edge = jnp.zeros((SUBLANES, width), F32)
    p_ref[0:pad] = zero_pad
    p_ref[pad + seq:2 * pad + seq] = zero_pad
    p_ref[pad:pad + seq] = u
    for ref in (a2_ref, a4_ref, a8_ref):
        ref[0:SUBLANES] = zero_edge
        ref[SUBLANES + span:2 * SUBLANES + span] = zero_edge
    a2_ref[SUBLANES:SUBLANES + span] = p_ref[7:7 + span] + p_ref[8:8 + span]
    a4_ref[SUBLANES:SUBLANES + span] = a2_ref[7:7 + span] + a2_ref[9:9 + span]
    a8_ref[SUBLANES:SUBLANES + span] = a4_ref[6:6 + span] + a4_ref[10:10 + span]
    s16 = a8_ref[pad - 4:pad - 4 + seq] + a8_ref[pad + 4:pad + 4 + seq]
    s2 = a2_ref[pad:pad + seq]
    s4 = a4_ref[pad:pad + seq]
    s8 = a8_ref[pad:pad + seq]
    lane = lax.broadcasted_iota(jnp.int32, (seq, width), 1)
    t = lax.broadcasted_iota(jnp.int32, (seq, width), 0)
    g0, g1, g2 = lane < POOL_GROUP, lane < 2 * POOL_GROUP, lane < 3 * POOL_GROUP
    half = jnp.where(g0, POOL_WINDOWS[0] // 2,
                     jnp.where(g1, POOL_WINDOWS[1] // 2,
                               jnp.where(g2, POOL_WINDOWS[2] // 2, POOL_WINDOWS[3] // 2)))
    count = (jnp.minimum(t + half, seq) - jnp.maximum(t - half, 0)).astype(F32)
    wsum = jnp.where(g0, s2, jnp.where(g1, s4, jnp.where(g2, s8, s16)))
    pooled = wsum / count - u
    o_ref[0] = (_dot(pooled.astype(BF16), w_ref[...]) * sc_ref[...]).astype(BF16)


def _pool_call(u, w_bd, scale):
    n_seq, lq, width = u.shape
    buf = pltpu.VMEM((lq + 2 * PAD_ROWS, width), F32)
    return pl.pallas_call(
        functools.partial(_pool_kernel, seq=lq),
        grid=(n_seq,),
        in_specs=[
            pl.BlockSpec((1, lq, width), lambda s: (s, 0, 0)),
            _resident((width, width)),
            _resident((1, width)),
        ],
        out_specs=pl.BlockSpec((1, lq, width), lambda s: (s, 0, 0)),
        out_shape=jax.ShapeDtypeStruct((n_seq, lq, width), BF16),
        scratch_shapes=[buf, buf, buf, buf],
        compiler_params=_params(1),
        name="pool",
    )(u, w_bd, scale.reshape(1, width))


def _dft_tables(b):
    n = 2 * b
    f = np.arange(b, dtype=np.int64)[:, None]
    s = np.arange(b, dtype=np.int64)[None, :]

    def packed(expo):
        ang = 2.0 * np.pi * ((f * expo) % n) / n
        w = np.concatenate([np.cos(ang), -np.sin(ang)], axis=0)
        w[b] = np.where((expo[0] % 2) == 0, 1.0, -1.0)
        return w

    fwd = packed(s)
    rev = packed(b - 1 - s)
    t = np.arange(b, dtype=np.int64)[:, None]
    fr = np.arange(b, dtype=np.int64)[None, :]
    ang = 2.0 * np.pi * ((fr * t) % n) / n
    inv = np.concatenate([2.0 * np.cos(ang), -2.0 * np.sin(ang)], axis=1) / n
    inv[:, 0] = 1.0 / n
    inv[:, b] = np.where((t[:, 0] % 2) == 0, 1.0, -1.0) / n
    return fwd.astype(np.float32), rev.astype(np.float32), inv.astype(np.float32)


def _hyena_tables(seq):
    t = np.linspace(0.0, 1.0, seq)[:, None]
    bands = (HYENA_EMB_DIM - 1) // 2
    w_ang = 2.0 * np.pi * np.arange(seq) / seq
    freqs = np.linspace(1e-4, bands - 1, bands)
    ang = w_ang[:, None] * freqs[None, :]
    z = np.concatenate([t, np.cos(ang), -np.sin(ang)], axis=-1)
    max_decay = math.log(HYENA_DECAY_TARGET) / HYENA_FAST_DECAY_PCT
    min_decay = math.log(HYENA_DECAY_TARGET) / HYENA_SLOW_DECAY_PCT
    deltas = np.linspace(min_decay, max_decay, HYENA_WIDTH)
    decay = np.exp(-t * np.abs(deltas))
    zz = np.zeros((seq, 2 * LANES))
    zz[:, :HYENA_EMB_DIM] = z
    zz[:-1, LANES:LANES + HYENA_EMB_DIM] = z[1:]
    decay_shift = np.zeros_like(decay)
    decay_shift[:-1] = decay[1:]
    return tuple(a.astype(np.float32) for a in (zz, decay, decay_shift))


def _dot_table(w_hi, w_lo, x):
    x_hi, x_lo = _split_bf16(x)
    return _dot(w_hi, x_hi) + _dot(w_hi, x_lo) + _dot(w_lo, x_hi)


def _filter_kernel(zz_ref, dec_ref, decs_ref, w1_ref, b1_ref, w2_ref, b2_ref, w3_ref, sf_ref,
                   fh_ref, fl_ref, rh_ref, rl_ref, o_ref, filt_ref, *, blk, nblk):
    order = pl.program_id(0)

    @pl.when(order == 0)
    def _():
        sf = sf_ref[...]
        h1 = jnp.sin(sf[0:1] * (_dot3(zz_ref[...], w1_ref[...]) + b1_ref[...]))
        h2 = jnp.sin(sf[1:2] * (_dot3(h1, w2_ref[...]) + b2_ref[...]))
        taps = _dot3(h2, w3_ref[...])
        for o in range(HYENA_ORDER):
            fwd = taps[:, o * HYENA_WIDTH:(o + 1) * HYENA_WIDTH] * dec_ref[...]
            bwd = (taps[:, (HYENA_ORDER + o) * HYENA_WIDTH:(HYENA_ORDER + o + 1) * HYENA_WIDTH]
                   * decs_ref[...])
            norm = (jnp.sum(jnp.abs(fwd), axis=0, keepdims=True)
                    + jnp.sum(jnp.abs(bwd), axis=0, keepdims=True))
            filt_ref[o] = fwd / norm
            filt_ref[HYENA_ORDER + o] = bwd / norm

    fwd = filt_ref[order]
    bwd = filt_ref[HYENA_ORDER + order]
    fh, fl, rh, rl = fh_ref[...], fl_ref[...], rh_ref[...], rl_ref[...]
    spec = {}
    for d in range(nblk):
        spec[d] = _dot_table(fh, fl, fwd[d * blk:(d + 1) * blk])
        spec[-(d + 1)] = _dot_table(rh, rl, bwd[d * blk:(d + 1) * blk])
    row = lax.broadcasted_iota(jnp.int32, (2 * blk, HYENA_WIDTH), 0)
    sign = (1 - 2 * (row & 1)).astype(F32)
    for d in range(-(nblk - 1), nblk):
        o_ref[0, d + nblk - 1] = spec[d] + sign * spec[d - 1]


def _filter_weights(w1, b1, w2, b2, w3, sf):
    fo = HYENA_FILTER_ORDER

    def block_diag(a, b):
        top = jnp.concatenate([a, jnp.zeros((a.shape[0], b.shape[1]), F32)], axis=1)
        bot = jnp.concatenate([jnp.zeros((b.shape[0], a.shape[1]), F32), b], axis=1)
        return jnp.concatenate([top, bot], axis=0)

    w1p = jnp.zeros((LANES, fo), F32).at[:HYENA_EMB_DIM].set(w1)
    w3r = w3.reshape(fo, HYENA_ORDER, 2, HYENA_WIDTH)
    w3f = w3r[:, :, 0].reshape(fo, HYENA_ORDER * HYENA_WIDTH)
    w3b = w3r[:, :, 1].reshape(fo, HYENA_ORDER * HYENA_WIDTH)
    return (block_diag(w1p, w1p), jnp.tile(b1, 2).reshape(1, 2 * fo),
            block_diag(w2, w2), jnp.tile(b2, 2).reshape(1, 2 * fo),
            block_diag(w3f, w3b), jnp.tile(sf, (1, 2)))


def _filter_call(tables, dft, weights, *, seq, blk):
    nblk = seq // blk
    zz, dec, decs = tables
    fh, fl, rh, rl = dft
    fo2 = 2 * HYENA_FILTER_ORDER
    n_taps = 2 * HYENA_ORDER * HYENA_WIDTH
    return pl.pallas_call(
        functools.partial(_filter_kernel, blk=blk, nblk=nblk),
        grid=(HYENA_ORDER,),
        in_specs=[
            _resident((seq, 2 * LANES)),
            _resident((seq, HYENA_WIDTH)), _resident((seq, HYENA_WIDTH)),
            _resident((2 * LANES, fo2)), _resident((1, fo2)),
            _resident((fo2, fo2)), _resident((1, fo2)),
            _resident((fo2, n_taps)),
            _resident((2, fo2)),
            _resident((2 * blk, blk)), _resident((2 * blk, blk)),
            _resident((2 * blk, blk)), _resident((2 * blk, blk)),
        ],
        out_specs=pl.BlockSpec((1, 2 * nblk - 1, 2 * blk, HYENA_WIDTH), lambda o: (o, 0, 0, 0)),
        out_shape=jax.ShapeDtypeStruct((HYENA_ORDER, 2 * nblk - 1, 2 * blk, HYENA_WIDTH), F32),
        scratch_shapes=[pltpu.VMEM((2 * HYENA_ORDER, seq, HYENA_WIDTH), F32)],
        compiler_params=_params(1),
        name="hyena_filter",
    )(zz, dec, decs, *weights, fh, fl, rh, rl)


def _hyena_kernel(p_ref, cw_ref, cb_ref, hb_ref, wf_ref, wi_ref, ch_ref, o_ref, pad_ref, *, seq, blk):
    nblk = seq // blk
    width = p_ref.shape[2]
    zero_edge = jnp.zeros((SUBLANES, width), F32)
    pad_ref[0:SUBLANES] = zero_edge
    pad_ref[SUBLANES + seq:2 * SUBLANES + seq] = zero_edge
    pad_ref[SUBLANES:SUBLANES + seq] = p_ref[0]
    cw = cw_ref[...]
    zc = (pad_ref[SUBLANES - 1:SUBLANES - 1 + seq] * cw[0:1]
          + pad_ref[SUBLANES:SUBLANES + seq] * cw[1:2]
          + pad_ref[SUBLANES + 1:SUBLANES + 1 + seq] * cw[2:3] + cb_ref[...])
    y = zc[:, :HYENA_WIDTH]
    wf = wf_ref[...]
    wi = wi_ref[...]
    hb = hb_ref[...]
    first = lax.broadcasted_iota(jnp.int32, (blk, HYENA_WIDTH), 0) == 0
    for o in range(HYENA_ORDER):
        gate = zc[:, (o + 1) * HYENA_WIDTH:(o + 2) * HYENA_WIDTH]
        spec = [_dot(wf, y[j * blk:(j + 1) * blk].astype(BF16)) for j in range(nblk)]
        outs = []
        for i in range(nblk):
            acc_r = jnp.zeros((blk, HYENA_WIDTH), F32)
            acc_i = jnp.zeros((blk, HYENA_WIDTH), F32)
            for j in range(nblk):
                cm = ch_ref[o, i - j + nblk - 1]
                ur, ui = spec[j][:blk], spec[j][blk:]
                cr, ci = cm[:blk], cm[blk:]
                ii = ui * ci
                acc_r = acc_r + ur * cr - jnp.where(first, 0.0, ii)
                acc_i = acc_i + jnp.where(first, ii, ur * ci + ui * cr)
            prod = jnp.concatenate([acc_r, acc_i], axis=0).astype(BF16)
            outs.append(_dot(wi, prod))
        conv = outs[0] if nblk == 1 else jnp.concatenate(outs, axis=0)
        y = gate * (conv + hb[o:o + 1] * y)
    o_ref[0] = y.astype(BF16)


def _hyena_call(p, conv_w, conv_b, hy_bias, wf, wi, chat, *, blk):
    n_seq, lq, width = p.shape
    nblk = lq // blk
    return pl.pallas_call(
        functools.partial(_hyena_kernel, seq=lq, blk=blk),
        grid=(n_seq,),
        in_specs=[
            pl.BlockSpec((1, lq, width), lambda s: (s, 0, 0)),
            _resident((3, width)),
            _resident((1, width)),
            _resident((HYENA_ORDER, HYENA_WIDTH)),
            _resident((2 * blk, blk)),
            _resident((blk, 2 * blk)),
            _resident((HYENA_ORDER, 2 * nblk - 1, 2 * blk, HYENA_WIDTH)),
        ],
        out_specs=pl.BlockSpec((1, lq, HYENA_WIDTH), lambda s: (s, 0, 0)),
        out_shape=jax.ShapeDtypeStruct((n_seq, lq, HYENA_WIDTH), BF16),
        scratch_shapes=[pltpu.VMEM((lq + 2 * SUBLANES, width), F32)],
        compiler_params=_params(1),
        name="hyena",
    )(p, conv_w, conv_b.reshape(1, width), hy_bias, wf, wi, chat)


def _qk(q, k):
    return lax.dot_general(q, k, (((1,), (1,)), ((), ())), preferred_element_type=F32)


def _attn_kernel(*refs, with_latent):
    if with_latent:
        q_ref, k_ref, v_ref, kc_ref, vc_ref, o_ref = refs
    else:
        q_ref, kc_ref, vc_ref, o_ref = refs
    q = q_ref[0]
    tq = q.shape[0]
    sub = min(Q_SUB, tq)
    pieces = [[None] * N_Q_HEADS for _ in range(tq // sub)]
    for kh in range(N_KV_HEADS):
        cols = slice(kh * HEAD_DIM, (kh + 1) * HEAD_DIM)
        k_c, v_c = kc_ref[0][:, cols], vc_ref[0][:, cols]
        if with_latent:
            k_l, v_l = k_ref[0][:, cols], v_ref[0][:, cols]
        for r in range(tq // sub):
            rows = slice(r * sub, (r + 1) * sub)
            qs = jnp.concatenate(
                [q[rows, (kh * GQA_GROUP + g) * HEAD_DIM:(kh * GQA_GROUP + g + 1) * HEAD_DIM]
                 for g in range(GQA_GROUP)], axis=0)
            s_c = _qk(qs, k_c)
            m = jnp.max(s_c, axis=-1, keepdims=True)
            if with_latent:
                s_l = _qk(qs, k_l)
                m = jnp.maximum(m, jnp.max(s_l, axis=-1, keepdims=True))
            p_c = jnp.exp2(s_c - m)
            den = jnp.sum(p_c, axis=-1, keepdims=True)
            o = _dot(p_c.astype(BF16), v_c)
            if with_latent:
                p_l = jnp.exp2(s_l - m)
                den = den + jnp.sum(p_l, axis=-1, keepdims=True)
                o = o + _dot(p_l.astype(BF16), v_l)
            o = o / den
            for g in range(GQA_GROUP):
                pieces[r][kh * GQA_GROUP + g] = o[g * sub:(g + 1) * sub]
    o_ref[0] = jnp.concatenate(
        [jnp.concatenate(row, axis=1) for row in pieces], axis=0).astype(BF16)


def _attn_call(q, kc, vc, k=None, v=None):
    n_seq, lq, _ = q.shape
    tq = min(Q_TILE, lq)
    ctx_len = kc.shape[1]
    with_latent = k is not None
    q_spec = pl.BlockSpec((1, tq, ATTN_WIDTH), lambda s, i: (s, i, 0))
    kv_spec = lambda n: pl.BlockSpec((1, n, KV_WIDTH), lambda s, i: (s, 0, 0))
    if with_latent:
        in_specs = [q_spec, kv_spec(lq), kv_spec(lq), kv_spec(ctx_len), kv_spec(ctx_len)]
        args = (q, k, v, kc, vc)
    else:
        in_specs = [q_spec, kv_spec(ctx_len), kv_spec(ctx_len)]
        args = (q, kc, vc)
    return pl.pallas_call(
        functools.partial(_attn_kernel, with_latent=with_latent),
        grid=(n_seq, lq // tq),
        in_specs=in_specs,
        out_specs=q_spec,
        out_shape=jax.ShapeDtypeStruct((n_seq, lq, ATTN_WIDTH), BF16),
        compiler_params=_params(2),
        name="attn" if with_latent else "attn_ctx",
    )(*args)


def _rope_tables(seq):
    rows = seq // GRID_W
    row = np.repeat(np.arange(rows), GRID_W).astype(np.float64)
    col = np.tile(np.arange(GRID_W), rows).astype(np.float64)
    inv_freq = 1.0 / (ROPE_THETA ** (np.arange(0, AXIS_DIM, 2, dtype=np.float64) / AXIS_DIM))
    ang_r = row[:, None] * inv_freq
    ang_c = col[:, None] * inv_freq
    ang = np.concatenate([ang_r, ang_r, ang_c, ang_c], axis=-1)
    cos, sin = np.cos(ang), np.sin(ang)
    upper = (np.arange(HEAD_DIM) % AXIS_DIM) >= AXIS_DIM // 2
    sin_a = np.where(upper, sin, 0.0)
    sin_b = np.where(upper, 0.0, -sin)
    rep = LANES // HEAD_DIM
    return tuple(jnp.asarray(np.tile(a, (1, rep)).astype(np.float32)) for a in (cos, sin_a, sin_b))


def _identity_tables(seq):
    one = jnp.ones((seq, LANES), F32)
    zero = jnp.zeros((seq, LANES), F32)
    return one, zero, zero


def _dft_operands(blk):
    fwd, rev, inv = (jnp.asarray(a) for a in _dft_tables(blk))
    fh, fl = _split_bf16(fwd)
    rh, rl = _split_bf16(rev)
    return (fh, fl, rh, rl), fh, inv.astype(BF16)


def kernel(x, c, ctx, c_ctx, norm_g, w_mod, b_mod, ffn_w_gate, ffn_w_up, ffn_w_down, w_in, w_out, pool_w, pool_scale, hyena_conv_w, hyena_conv_b, hyena_f_w1, hyena_f_b1, hyena_f_w2, hyena_f_b2, hyena_f_w3, hyena_sin_freq, hyena_bias, q_norm_g, k_norm_g):
    batch, seq, d = x.shape
    ctx_len = ctx.shape[1]
    depth = w_mod.shape[0]

    rows = -(-(batch + 1) // SUBLANES) * SUBLANES
    cvec = jnp.zeros((rows, d), F32).at[:batch].set(c).at[batch].set(c_ctx)
    mods = _mods_call(cvec, w_mod, b_mod).reshape(depth * rows, N_MOD, d)

    blk_x = min(DFT_BLOCK, seq)
    blk_c = min(DFT_BLOCK, ctx_len)
    dft_x = _dft_operands(blk_x)
    dft_c = _dft_operands(blk_c)
    hy_tab_x = tuple(jnp.asarray(a) for a in _hyena_tables(seq))
    hy_tab_c = tuple(jnp.asarray(a) for a in _hyena_tables(ctx_len))
    rope_x = _rope_tables(seq)
    rope_c = _identity_tables(ctx_len)
    seg = np.arange(LANES) // HEAD_DIM
    seg_mean = jnp.asarray((seg[:, None] == seg[None, :]).astype(np.float32) / HEAD_DIM).astype(BF16)
    n_win = len(POOL_WINDOWS)

    h, hc = x, ctx
    for l in range(depth):
        last = l == depth - 1
        row_x = lambda s, l=l: l * rows + s
        row_c = lambda s, l=l: l * rows + batch
        f1 = tuple(w[l, 0].astype(BF16) for w in (ffn_w_gate, ffn_w_up, ffn_w_down))
        f2 = tuple(w[l, 1].astype(BF16) for w in (ffn_w_gate, ffn_w_up, ffn_w_down))
        w_in_l = w_in[l].astype(BF16)
        w_out_l = w_out[l].astype(BF16)
        w_pool = jnp.zeros((n_win, POOL_GROUP, n_win, POOL_GROUP), F32)
        for gi in range(n_win):
            w_pool = w_pool.at[gi, :, gi, :].set(pool_w[l, gi])
        w_pool = w_pool.reshape(POOL_WIDTH, POOL_WIDTH).astype(BF16)
        qg = jnp.tile(q_norm_g[l], LANES // HEAD_DIM).reshape(1, LANES)
        kg = jnp.tile(k_norm_g[l], LANES // HEAD_DIM).reshape(1, LANES)
        filt_w = _filter_weights(hyena_f_w1[l], hyena_f_b1[l], hyena_f_w2[l], hyena_f_b2[l],
                                 hyena_f_w3[l], hyena_sin_freq[l])

        h, pool_x, hy_x, q_x, k_x, v_x = _ffn_in_call(
            h, mods, norm_g[l, 0], f1, norm_g[l, 1], w_in_l, rope_x, qg, kg, seg_mean, mod_row=row_x)
        hc, pool_c, hy_c, q_c, k_c, v_c = _ffn_in_call(
            hc, mods, norm_g[l, 0], f1, norm_g[l, 1], w_in_l, rope_c, qg, kg, seg_mean, mod_row=row_c)

        chat_x = _filter_call(hy_tab_x, dft_x[0], filt_w, seq=seq, blk=blk_x)
        mix_x = (
            _pool_call(pool_x, w_pool, pool_scale[l]),
            _hyena_call(hy_x, hyena_conv_w[l], hyena_conv_b[l], hyena_bias[l], dft_x[1], dft_x[2],
                        chat_x, blk=blk_x),
            _attn_call(q_x, k_c, v_c, k_x, v_x),
        )
        if not last:
            chat_c = _filter_call(hy_tab_c, dft_c[0], filt_w, seq=ctx_len, blk=blk_c)
            mix_c = (
                _pool_call(pool_c, w_pool, pool_scale[l]),
                _hyena_call(hy_c, hyena_conv_w[l], hyena_conv_b[l], hyena_bias[l], dft_c[1], dft_c[2],
                            chat_c, blk=blk_c),
                _attn_call(q_c, k_c, v_c),
            )
            hc = _ffn_out_call(hc, mods, mix_c, w_out_l, norm_g[l, 2], f2, mod_row=row_c)
        h = _ffn_out_call(h, mods, mix_x, w_out_l, norm_g[l, 2], f2, mod_row=row_x)
    return h
```

```python
import functools
import math

import jax
import jax.numpy as jnp
import numpy as np
from jax import lax
from jax.experimental import pallas as pl
from jax.experimental.pallas import tpu as pltpu

F32 = jnp.float32
BF16 = jnp.bfloat16

GRID_W = 64
EPS = 1e-6
N_MOD = 9
POOL_WIDTH = 256
POOL_WINDOWS = (2, 4, 8, 16)
POOL_GROUP = POOL_WIDTH // len(POOL_WINDOWS)
HYENA_WIDTH = 256
HYENA_ORDER = 2
HYENA_EMB_DIM = 33
HYENA_FILTER_ORDER = 64
HYENA_DECAY_TARGET = 1e-2
HYENA_FAST_DECAY_PCT = 0.3
HYENA_SLOW_DECAY_PCT = 1.5
HEAD_DIM = 64
N_Q_HEADS = 8
N_KV_HEADS = 2
GQA_GROUP = N_Q_HEADS // N_KV_HEADS
ATTN_WIDTH = N_Q_HEADS * HEAD_DIM
KV_WIDTH = N_KV_HEADS * HEAD_DIM
AXIS_DIM = HEAD_DIM // 2
ROPE_THETA = 10000.0
ATTN_SCALE = HEAD_DIM ** -0.5
Q_PRESCALE = ATTN_SCALE * math.log2(math.e)
HYENA_OFF = POOL_WIDTH
Q_OFF = HYENA_OFF + (HYENA_ORDER + 1) * HYENA_WIDTH
K_OFF = Q_OFF + ATTN_WIDTH
V_OFF = K_OFF + KV_WIDTH
IN_WIDTH = V_OFF + KV_WIDTH

LANES = 128
SUBLANES = 8
VMEM_LIMIT_BYTES = 56 * 1024 * 1024

TOKEN_TILE = 512
Q_TILE = 512
Q_SUB = 128
DFT_BLOCK = 512
MOD_COL_TILE = 1152
PAD_ROWS = 16


def _params(n_axes):
    return pltpu.CompilerParams(
        dimension_semantics=("arbitrary",) * n_axes, vmem_limit_bytes=VMEM_LIMIT_BYTES)


def _resident(shape, lead=()):
    index = tuple(lead) + (0,) * len(shape)
    return pl.BlockSpec((None,) * len(lead) + tuple(shape), lambda *_: index,
                        pipeline_mode=pl.Buffered(1))


def _split_bf16(a):
    hi = a.astype(BF16)
    lo = (a - hi.astype(F32)).astype(BF16)
    return hi, lo


def _dot(a, b):
    return jnp.dot(a, b, preferred_element_type=F32)


def _dot3(a, b):
    a_hi, a_lo = _split_bf16(a)
    b_hi, b_lo = _split_bf16(b)
    return _dot(a_hi, b_hi) + _dot(a_lo, b_hi) + _dot(a_hi, b_lo)


def _norm_mod(x, g, shift, scale):
    ms = jnp.mean(x * x, axis=-1, keepdims=True)
    y = x * lax.rsqrt(ms + EPS) * g
    return y * (1.0 + scale) + shift


def _mods_kernel(c_ref, w_ref, b_ref, o_ref):
    rows = c_ref.shape[0]
    cc = c_ref[...]
    act = cc * jax.nn.sigmoid(cc)
    a_hi, a_lo = _split_bf16(act)
    w_hi, w_lo = _split_bf16(w_ref[0])
    r = _dot(jnp.concatenate([a_hi, a_lo], axis=0), w_hi)
    o_ref[0] = r[:rows] + r[rows:] + _dot(a_hi, w_lo) + b_ref[0]


def _mods_call(cvec, w_mod, b_mod):
    depth, d, n = w_mod.shape
    rows = cvec.shape[0]
    tn = MOD_COL_TILE
    return pl.pallas_call(
        _mods_kernel,
        grid=(depth, n // tn),
        in_specs=[
            pl.BlockSpec((rows, d), lambda l, j: (0, 0)),
            pl.BlockSpec((1, d, tn), lambda l, j: (l, 0, j)),
            pl.BlockSpec((1, 1, tn), lambda l, j: (l, 0, j)),
        ],
        out_specs=pl.BlockSpec((1, rows, tn), lambda l, j: (l, 0, j)),
        out_shape=jax.ShapeDtypeStruct((depth, rows, n), F32),
        compiler_params=_params(2),
        name="mods",
    )(cvec, w_mod, b_mod.reshape(depth, 1, n))


FF_CHUNK = 1024


def _ffn_body(x, m, g, wg_ref, wu_ref, wd_ref, mod_off):
    y = _norm_mod(x, g, m[mod_off:mod_off + 1], m[mod_off + 1:mod_off + 2]).astype(BF16)
    d_ff = wg_ref.shape[1]
    acc = jnp.zeros(x.shape, F32)
    for lo in range(0, d_ff, FF_CHUNK):
        hi = min(lo + FF_CHUNK, d_ff)
        a = _dot(y, wg_ref[:, lo:hi])
        u = _dot(y, wu_ref[:, lo:hi])
        mid = (a * jax.nn.sigmoid(a) * u).astype(BF16)
        acc = acc + _dot(mid, wd_ref[lo:hi, :])
    return x + 0.5 * m[mod_off + 2:mod_off + 3] * acc


def _head_norm_rope(x, g, seg_mean, cos, sin_a, sin_b):
    sq_hi, sq_lo = _split_bf16(x * x)
    ms = _dot(sq_hi, seg_mean) + _dot(sq_lo, seg_mean)
    y = x * lax.rsqrt(ms + EPS) * g
    return (y * cos + pltpu.roll(y, AXIS_DIM // 2, 1) * sin_a
            + pltpu.roll(y, LANES - AXIS_DIM // 2, 1) * sin_b)


def _ffn_in_kernel(x_ref, mod_ref, g0_ref, wg_ref, wu_ref, wd_ref, g1_ref, w_ref,
                   cos_ref, sa_ref, sb_ref, qg_ref, kg_ref, sm_ref,
                   h_ref, pool_ref, hy_ref, q_ref, k_ref, vt_ref):
    m = mod_ref[0]
    h = _ffn_body(x_ref[0], m, g0_ref[...], wg_ref, wu_ref, wd_ref, 0)
    h_ref[0] = h
    y = _norm_mod(h, g1_ref[...], m[3:4], m[4:5]).astype(BF16)
    p = _dot(y, w_ref[...])
    pool_ref[0] = p[:, :HYENA_OFF]
    hy_ref[0] = p[:, HYENA_OFF:Q_OFF]
    cos, sa, sb, sm = cos_ref[...], sa_ref[...], sb_ref[...], sm_ref[...]
    for j in range(ATTN_WIDTH // LANES):
        qj = p[:, Q_OFF + j * LANES:Q_OFF + (j + 1) * LANES]
        qj = (_head_norm_rope(qj, qg_ref[...], sm, cos, sa, sb) * Q_PRESCALE).astype(BF16)
        q_ref[0, 2 * j] = qj[:, :HEAD_DIM]
        q_ref[0, 2 * j + 1] = qj[:, HEAD_DIM:]
    kk = _head_norm_rope(p[:, K_OFF:V_OFF], kg_ref[...], sm, cos, sa, sb).astype(BF16)
    k_ref[0, 0] = kk[:, :HEAD_DIM]
    k_ref[0, 1] = kk[:, HEAD_DIM:]
    vt_ref[0] = p[:, V_OFF:].T.astype(BF16)


def _ffn_in_call(h, mods, g0, ffn_w, g1, w_in, tables, qg, kg, seg_mean, *, mod_row, layer, half):
    n_seq, lq, d = h.shape
    tm = min(TOKEN_TILE, lq)
    wg, wu, wd = ffn_w
    d_ff = wg.shape[-1]
    cos, sa, sb = tables
    tab_spec = pl.BlockSpec((tm, LANES), lambda s, i: (i, 0))
    tok = lambda w: pl.BlockSpec((1, tm, w), lambda s, i: (s, i, 0))
    shp = lambda w, dt: jax.ShapeDtypeStruct((n_seq, lq, w), dt)
    return pl.pallas_call(
        _ffn_in_kernel,
        grid=(n_seq, lq // tm),
        in_specs=[
            tok(d),
            pl.BlockSpec((1, N_MOD, d), lambda s, i: (mod_row(s), 0, 0)),
            _resident((1, d)),
            _resident((d, d_ff), (layer, half)), _resident((d, d_ff), (layer, half)),
            _resident((d_ff, d), (layer, half)),
            _resident((1, d)),
            _resident((d, IN_WIDTH), (layer,)),
            tab_spec, tab_spec, tab_spec,
            _resident((1, LANES)), _resident((1, LANES)), _resident((LANES, LANES)),
        ],
        out_specs=[tok(d), tok(POOL_WIDTH), tok(Q_OFF - HYENA_OFF),
                   pl.BlockSpec((1, N_Q_HEADS, tm, HEAD_DIM), lambda s, i: (s, 0, i, 0)),
                   pl.BlockSpec((1, N_KV_HEADS, tm, HEAD_DIM), lambda s, i: (s, 0, i, 0)),
                   pl.BlockSpec((1, KV_WIDTH, tm), lambda s, i: (s, 0, i))],
        out_shape=[shp(d, F32), shp(POOL_WIDTH, F32), shp(Q_OFF - HYENA_OFF, F32),
                   jax.ShapeDtypeStruct((n_seq, N_Q_HEADS, lq, HEAD_DIM), BF16),
                   jax.ShapeDtypeStruct((n_seq, N_KV_HEADS, lq, HEAD_DIM), BF16),
                   jax.ShapeDtypeStruct((n_seq, KV_WIDTH, lq), BF16)],
        compiler_params=_params(2),
        name="ffn_in",
    )(h, mods, g0.reshape(1, d), wg, wu, wd, g1.reshape(1, d), w_in, cos, sa, sb, qg, kg, seg_mean)


def _ffn_out_kernel(h_ref, mod_ref, a_ref, b_ref, c_ref, w_ref, g_ref, wg_ref, wu_ref, wd_ref, o_ref):
    m = mod_ref[0]
    wa = a_ref.shape[2]
    wb = b_ref.shape[2]
    mix = (_dot(a_ref[0], w_ref[0:wa, :]) + _dot(b_ref[0], w_ref[wa:wa + wb, :])
           + _dot(c_ref[0], w_ref[wa + wb:, :]))
    h = h_ref[0] + m[5:6] * mix
    o_ref[0] = _ffn_body(h, m, g_ref[...], wg_ref, wu_ref, wd_ref, 6)


def _ffn_out_call(h, mods, mix, w_out, g, ffn_w, *, mod_row, layer, half):
    n_seq, lq, d = h.shape
    tm = min(TOKEN_TILE, lq)
    wg, wu, wd = ffn_w
    d_ff = wg.shape[-1]
    tok = lambda w: pl.BlockSpec((1, tm, w), lambda s, i: (s, i, 0))
    return pl.pallas_call(
        _ffn_out_kernel,
        grid=(n_seq, lq // tm),
        in_specs=[
            tok(d),
            pl.BlockSpec((1, N_MOD, d), lambda s, i: (mod_row(s), 0, 0)),
            tok(mix[0].shape[2]), tok(mix[1].shape[2]), tok(mix[2].shape[2]),
            _resident(w_out.shape[1:], (layer,)),
            _resident((1, d)),
            _resident((d, d_ff), (layer, half)), _resident((d, d_ff), (layer, half)),
            _resident((d_ff, d), (layer, half)),
        ],
        out_specs=tok(d),
        out_shape=jax.ShapeDtypeStruct((n_seq, lq, d), F32),
        compiler_params=_params(2),
        name="ffn_out",
    )(h, mods, *mix, w_out, g.reshape(1, d), wg, wu, wd)


def _pool_kernel(u_ref, w_ref, sc_ref, o_ref, p_ref, a2_ref, a4_ref, a8_ref, *, seq):
    u = u_ref[0]
    width = u.shape[1]
    pad = PAD_ROWS
    span = seq + pad
    zero_pad = jnp.zeros((pad, width), F32)
    zero_edge = jnp.zeros((SUBLANES, width), F32)
    p_ref[0:pad] = zero_pad
    p_ref[pad + seq:2 * pad + seq] = zero_pad
    p_ref[pad:pad + seq] = u
    for ref in (a2_ref, a4_ref, a8_ref):
        ref[0:SUBLANES] = zero_edge
        ref[SUBLANES + span:2 * SUBLANES + span] = zero_edge
    a2_ref[SUBLANES:SUBLANES + span] = p_ref[7:7 + span] + p_ref[8:8 + span]
    a4_ref[SUBLANES:SUBLANES + span] = a2_ref[7:7 + span] + a2_ref[9:9 + span]
    a8_ref[SUBLANES:SUBLANES + span] = a4_ref[6:6 + span] + a4_ref[10:10 + span]
    s16 = a8_ref[pad - 4:pad - 4 + seq] + a8_ref[pad + 4:pad + 4 + seq]
    s2 = a2_ref[pad:pad + seq]
    s4 = a4_ref[pad:pad + seq]
    s8 = a8_ref[pad:pad + seq]
    lane = lax.broadcasted_iota(jnp.int32, (seq, width), 1)
    t = lax.broadcasted_iota(jnp.int32, (seq, width), 0)
    g0, g1, g2 = lane < POOL_GROUP, lane < 2 * POOL_GROUP, lane < 3 * POOL_GROUP
    half = jnp.where(g0, POOL_WINDOWS[0] // 2,
                     jnp.where(g1, POOL_WINDOWS[1] // 2,
                               jnp.where(g2, POOL_WINDOWS[2] // 2, POOL_WINDOWS[3] // 2)))
    count = (jnp.minimum(t + half, seq) - jnp.maximum(t - half, 0)).astype(F32)
    wsum = jnp.where(g0, s2, jnp.where(g1, s4, jnp.where(g2, s8, s16)))
    pooled = wsum / count - u
    o_ref[0] = (_dot(pooled.astype(BF16), w_ref[...]) * sc_ref[...]).astype(BF16)


def _pool_call(u, w_bd, scale):
    n_seq, lq, width = u.shape
    buf = pltpu.VMEM((lq + 2 * PAD_ROWS, width), F32)
    return pl.pallas_call(
        functools.partial(_pool_kernel, seq=lq),
        grid=(n_seq,),
        in_specs=[
            pl.BlockSpec((1, lq, width), lambda s: (s, 0, 0)),
            _resident((width, width)),
            _resident((1, width)),
        ],
        out_specs=pl.BlockSpec((1, lq, width), lambda s: (s, 0, 0)),
        out_shape=jax.ShapeDtypeStruct((n_seq, lq, width), BF16),
        scratch_shapes=[buf, buf, buf, buf],
        compiler_params=_params(1),
        name="pool",
    )(u, w_bd, scale.reshape(1, width))


def _dft_tables(b):
    n = 2 * b
    f = np.arange(b, dtype=np.int64)[:, None]
    s = np.arange(b, dtype=np.int64)[None, :]

    def packed(expo):
        ang = 2.0 * np.pi * ((f * expo) % n) / n
        w = np.concatenate([np.cos(ang), -np.sin(ang)], axis=0)
        w[b] = np.where((expo[0] % 2) == 0, 1.0, -1.0)
        return w

    fwd = packed(s)
    rev = packed(b - 1 - s)
    t = np.arange(b, dtype=np.int64)[:, None]
    fr = np.arange(b, dtype=np.int64)[None, :]
    ang = 2.0 * np.pi * ((fr * t) % n) / n
    inv = np.concatenate([2.0 * np.cos(ang), -2.0 * np.sin(ang)], axis=1) / n
    inv[:, 0] = 1.0 / n
    inv[:, b] = np.where((t[:, 0] % 2) == 0, 1.0, -1.0) / n
    return fwd.astype(np.float32), rev.astype(np.float32), inv.astype(np.float32)


def _hyena_tables(seq):
    t = np.linspace(0.0, 1.0, seq)[:, None]
    bands = (HYENA_EMB_DIM - 1) // 2
    w_ang = 2.0 * np.pi * np.arange(seq) / seq
    freqs = np.linspace(1e-4, bands - 1, bands)
    ang = w_ang[:, None] * freqs[None, :]
    z = np.concatenate([t, np.cos(ang), -np.sin(ang)], axis=-1)
    max_decay = math.log(HYENA_DECAY_TARGET) / HYENA_FAST_DECAY_PCT
    min_decay = math.log(HYENA_DECAY_TARGET) / HYENA_SLOW_DECAY_PCT
    deltas = np.linspace(min_decay, max_decay, HYENA_WIDTH)
    decay = np.exp(-t * np.abs(deltas))
    zz = np.zeros((seq, 2 * LANES))
    zz[:, :HYENA_EMB_DIM] = z
    zz[:-1, LANES:LANES + HYENA_EMB_DIM] = z[1:]
    decay_shift = np.zeros_like(decay)
    decay_shift[:-1] = decay[1:]
    return tuple(a.astype(np.float32) for a in (zz, decay, decay_shift))


def _dot_table(w_hi, w_lo, x):
    x_hi, x_lo = _split_bf16(x)
    return _dot(w_hi, x_hi) + _dot(w_hi, x_lo) + _dot(w_lo, x_hi)


def _filter_kernel(zz_ref, dec_ref, decs_ref, w1_ref, b1_ref, w2_ref, b2_ref, w3_ref, sf_ref,
                   fh_ref, fl_ref, rh_ref, rl_ref, o_ref, filt_ref, *, blk, nblk):
    order = pl.program_id(0)

    @pl.when(order == 0)
    def _():
        sf = sf_ref[...]
        h1 = jnp.sin(sf[0:1] * (_dot3(zz_ref[...], w1_ref[...]) + b1_ref[...]))
        h2 = jnp.sin(sf[1:2] * (_dot3(h1, w2_ref[...]) + b2_ref[...]))
        taps = _dot3(h2, w3_ref[...])
        for o in range(HYENA_ORDER):
            fwd = taps[:, o * HYENA_WIDTH:(o + 1) * HYENA_WIDTH] * dec_ref[...]
            bwd = (taps[:, (HYENA_ORDER + o) * HYENA_WIDTH:(HYENA_ORDER + o + 1) * HYENA_WIDTH]
                   * decs_ref[...])
            norm = (jnp.sum(jnp.abs(fwd), axis=0, keepdims=True)
                    + jnp.sum(jnp.abs(bwd), axis=0, keepdims=True))
            filt_ref[o] = fwd / norm
            filt_ref[HYENA_ORDER + o] = bwd / norm

    fwd = filt_ref[order]
    bwd = filt_ref[HYENA_ORDER + order]
    fh, fl, rh, rl = fh_ref[...], fl_ref[...], rh_ref[...], rl_ref[...]
    spec = {}
    for d in range(nblk):
        spec[d] = _dot_table(fh, fl, fwd[d * blk:(d + 1) * blk])
        spec[-(d + 1)] = _dot_table(rh, rl, bwd[d * blk:(d + 1) * blk])
    row = lax.broadcasted_iota(jnp.int32, (2 * blk, HYENA_WIDTH), 0)
    sign = (1 - 2 * (row & 1)).astype(F32)
    for d in range(-(nblk - 1), nblk):
        o_ref[0, d + nblk - 1] = spec[d] + sign * spec[d - 1]


def _filter_weights(w1, b1, w2, b2, w3, sf):
    fo = HYENA_FILTER_ORDER

    def block_diag(a, b):
        top = jnp.concatenate([a, jnp.zeros((a.shape[0], b.shape[1]), F32)], axis=1)
        bot = jnp.concatenate([jnp.zeros((b.shape[0], a.shape[1]), F32), b], axis=1)
        return jnp.concatenate([top, bot], axis=0)

    w1p = jnp.zeros((LANES, fo), F32).at[:HYENA_EMB_DIM].set(w1)
    w3r = w3.reshape(fo, HYENA_ORDER, 2, HYENA_WIDTH)
    w3f = w3r[:, :, 0].reshape(fo, HYENA_ORDER * HYENA_WIDTH)
    w3b = w3r[:, :, 1].reshape(fo, HYENA_ORDER * HYENA_WIDTH)
    return (block_diag(w1p, w1p), jnp.tile(b1, 2).reshape(1, 2 * fo),
            block_diag(w2, w2), jnp.tile(b2, 2).reshape(1, 2 * fo),
            block_diag(w3f, w3b), jnp.tile(sf, (1, 2)))


def _filter_call(tables, dft, weights, *, seq, blk):
    nblk = seq // blk
    zz, dec, decs = tables
    fh, fl, rh, rl = dft
    fo2 = 2 * HYENA_FILTER_ORDER
    n_taps = 2 * HYENA_ORDER * HYENA_WIDTH
    return pl.pallas_call(
        functools.partial(_filter_kernel, blk=blk, nblk=nblk),
        grid=(HYENA_ORDER,),
        in_specs=[
            _resident((seq, 2 * LANES)),
            _resident((seq, HYENA_WIDTH)), _resident((seq, HYENA_WIDTH)),
            _resident((2 * LANES, fo2)), _resident((1, fo2)),
            _resident((fo2, fo2)), _resident((1, fo2)),
            _resident((fo2, n_taps)),
            _resident((2, fo2)),
            _resident((2 * blk, blk)), _resident((2 * blk, blk)),
            _resident((2 * blk, blk)), _resident((2 * blk, blk)),
        ],
        out_specs=pl.BlockSpec((1, 2 * nblk - 1, 2 * blk, HYENA_WIDTH), lambda o: (o, 0, 0, 0)),
        out_shape=jax.ShapeDtypeStruct((HYENA_ORDER, 2 * nblk - 1, 2 * blk, HYENA_WIDTH), F32),
        scratch_shapes=[pltpu.VMEM((2 * HYENA_ORDER, seq, HYENA_WIDTH), F32)],
        compiler_params=_params(1),
        name="hyena_filter",
    )(zz, dec, decs, *weights, fh, fl, rh, rl)


def _short_conv3(p, cw, cb):
    seq, width = p.shape
    zc = pltpu.roll(p, 1, 0) * cw[0:1] + p * cw[1:2] + pltpu.roll(p, seq - 1, 0) * cw[2:3] + cb
    row = lax.broadcasted_iota(jnp.int32, (SUBLANES, width), 0)
    head = zc[:SUBLANES] - jnp.where(row == 0, p[seq - 1:seq] * cw[0:1], 0.0)
    tail = zc[seq - SUBLANES:] - jnp.where(row == SUBLANES - 1, p[0:1] * cw[2:3], 0.0)
    return jnp.concatenate([head, zc[SUBLANES:seq - SUBLANES], tail], axis=0)


def _hyena_kernel(p_ref, cw_ref, cb_ref, hb_ref, wf_ref, wi_ref, ch_ref, o_ref, *, seq, blk):
    nblk = seq // blk
    zc = _short_conv3(p_ref[0], cw_ref[...], cb_ref[...])
    y = zc[:, :HYENA_WIDTH]
    wf = wf_ref[...]
    wi = wi_ref[...]
    hb = hb_ref[...]
    first = lax.broadcasted_iota(jnp.int32, (blk, HYENA_WIDTH), 0) == 0
    for o in range(HYENA_ORDER):
        gate = zc[:, (o + 1) * HYENA_WIDTH:(o + 2) * HYENA_WIDTH]
        spec = [_dot(wf, y[j * blk:(j + 1) * blk].astype(BF16)) for j in range(nblk)]
        outs = []
        for i in range(nblk):
            acc_r = jnp.zeros((blk, HYENA_WIDTH), F32)
            acc_i = jnp.zeros((blk, HYENA_WIDTH), F32)
            for j in range(nblk):
                cm = ch_ref[o, i - j + nblk - 1]
                ur, ui = spec[j][:blk], spec[j][blk:]
                cr, ci = cm[:blk], cm[blk:]
                ii = ui * ci
                acc_r = acc_r + ur * cr - jnp.where(first, 0.0, ii)
                acc_i = acc_i + jnp.where(first, ii, ur * ci + ui * cr)
            prod = jnp.concatenate([acc_r, acc_i], axis=0).astype(BF16)
            outs.append(_dot(wi, prod))
        conv = outs[0] if nblk == 1 else jnp.concatenate(outs, axis=0)
        y = gate * (conv + hb[o:o + 1] * y)
    o_ref[0] = y.astype(BF16)


def _hyena_call(p, conv_w, conv_b, hy_bias, wf, wi, chat, *, blk):
    n_seq, lq, width = p.shape
    nblk = lq // blk
    return pl.pallas_call(
        functools.partial(_hyena_kernel, seq=lq, blk=blk),
        grid=(n_seq,),
        in_specs=[
            pl.BlockSpec((1, lq, width), lambda s: (s, 0, 0)),
            _resident((3, width)),
            _resident((1, width)),
            _resident((HYENA_ORDER, HYENA_WIDTH)),
            _resident((2 * blk, blk)),
            _resident((blk, 2 * blk)),
            _resident((HYENA_ORDER, 2 * nblk - 1, 2 * blk, HYENA_WIDTH)),
        ],
        out_specs=pl.BlockSpec((1, lq, HYENA_WIDTH), lambda s: (s, 0, 0)),
        out_shape=jax.ShapeDtypeStruct((n_seq, lq, HYENA_WIDTH), BF16),
        compiler_params=_params(1),
        name="hyena",
    )(p, conv_w, conv_b.reshape(1, width), hy_bias, wf, wi, chat)


def _qk(q, k):
    return lax.dot_general(q, k, (((1,), (1,)), ((), ())), preferred_element_type=F32)


def _attn_kernel(*refs, with_latent):
    if with_latent:
        q_ref, k_ref, vt_ref, kc_ref, vct_ref, o_ref = refs
        sources = ((k_ref, vt_ref), (kc_ref, vct_ref))
    else:
        q_ref, kc_ref, vct_ref, o_ref = refs
        sources = ((kc_ref, vct_ref),)
    tq = q_ref.shape[2]
    sub = min(Q_SUB, tq)
    row_blocks = []
    for r in range(tq // sub):
        rows = slice(r * sub, (r + 1) * sub)
        head_outs = [None] * N_Q_HEADS
        for kh in range(N_KV_HEADS):
            feat = slice(kh * HEAD_DIM, (kh + 1) * HEAD_DIM)
            heads = range(kh * GQA_GROUP, (kh + 1) * GQA_GROUP)
            qs = jnp.concatenate([q_ref[0, h, rows, :] for h in heads], axis=0)
            scores = [_qk(key_ref[0, kh], qs) for key_ref, _ in sources]
            m = functools.reduce(jnp.maximum, [jnp.max(s, axis=0, keepdims=True) for s in scores])
            den = acc = None
            for s, (_, val_ref) in zip(scores, sources):
                p = jnp.exp2(s - m)
                part_den = jnp.sum(p, axis=0, keepdims=True)
                part = _dot(val_ref[0, feat, :], p.astype(BF16))
                den = part_den if den is None else den + part_den
                acc = part if acc is None else acc + part
            o_t = acc / den
            for g, h in enumerate(heads):
                head_outs[h] = o_t[:, g * sub:(g + 1) * sub]
        row_blocks.append(jnp.concatenate(head_outs, axis=0).T)
    o_ref[0] = jnp.concatenate(row_blocks, axis=0).astype(BF16)


def _attn_call(q, kc, vct, k=None, vt=None):
    n_seq, _, lq, _ = q.shape
    tq = min(Q_TILE, lq)
    ctx_len = kc.shape[2]
    with_latent = k is not None
    q_spec = pl.BlockSpec((1, N_Q_HEADS, tq, HEAD_DIM), lambda s, i: (s, 0, i, 0))
    k_spec = lambda n: pl.BlockSpec((1, N_KV_HEADS, n, HEAD_DIM), lambda s, i: (s, 0, 0, 0))
    vt_spec = lambda n: pl.BlockSpec((1, KV_WIDTH, n), lambda s, i: (s, 0, 0))
    if with_latent:
        in_specs = [q_spec, k_spec(lq), vt_spec(lq), k_spec(ctx_len), vt_spec(ctx_len)]
        args = (q, k, vt, kc, vct)
    else:
        in_specs = [q_spec, k_spec(ctx_len), vt_spec(ctx_len)]
        args = (q, kc, vct)
    return pl.pallas_call(
        functools.partial(_attn_kernel, with_latent=with_latent),
        grid=(n_seq, lq // tq),
        in_specs=in_specs,
        out_specs=pl.BlockSpec((1, tq, ATTN_WIDTH), lambda s, i: (s, i, 0)),
        out_shape=jax.ShapeDtypeStruct((n_seq, lq, ATTN_WIDTH), BF16),
        compiler_params=_params(2),
        name="attn" if with_latent else "attn_ctx",
    )(*args)


def _rope_tables(seq):
    rows = seq // GRID_W
    row = np.repeat(np.arange(rows), GRID_W).astype(np.float64)
    col = np.tile(np.arange(GRID_W), rows).astype(np.float64)
    inv_freq = 1.0 / (ROPE_THETA ** (np.arange(0, AXIS_DIM, 2, dtype=np.float64) / AXIS_DIM))
    ang_r = row[:, None] * inv_freq
    ang_c = col[:, None] * inv_freq
    ang = np.concatenate([ang_r, ang_r, ang_c, ang_c], axis=-1)
    cos, sin = np.cos(ang), np.sin(ang)
    upper = (np.arange(HEAD_DIM) % AXIS_DIM) >= AXIS_DIM // 2
    sin_a = np.where(upper, sin, 0.0)
    sin_b = np.where(upper, 0.0, -sin)
    rep = LANES // HEAD_DIM
    return tuple(jnp.asarray(np.tile(a, (1, rep)).astype(np.float32)) for a in (cos, sin_a, sin_b))


def _identity_tables(seq):
    one = jnp.ones((seq, LANES), F32)
    zero = jnp.zeros((seq, LANES), F32)
    return one, zero, zero


def _dft_operands(blk):
    fwd, rev, inv = (jnp.asarray(a) for a in _dft_tables(blk))
    fh, fl = _split_bf16(fwd)
    rh, rl = _split_bf16(rev)
    return (fh, fl, rh, rl), fh, inv.astype(BF16)


def kernel(x, c, ctx, c_ctx, norm_g, w_mod, b_mod, ffn_w_gate, ffn_w_up, ffn_w_down, w_in, w_out, pool_w, pool_scale, hyena_conv_w, hyena_conv_b, hyena_f_w1, hyena_f_b1, hyena_f_w2, hyena_f_b2, hyena_f_w3, hyena_sin_freq, hyena_bias, q_norm_g, k_norm_g):
    batch, seq, d = x.shape
    ctx_len = ctx.shape[1]
    depth = w_mod.shape[0]

    rows = -(-(batch + 1) // SUBLANES) * SUBLANES
    cvec = jnp.zeros((rows, d), F32).at[:batch].set(c).at[batch].set(c_ctx)
    mods = _mods_call(cvec, w_mod, b_mod).reshape(depth * rows, N_MOD, d)

    blk_x = min(DFT_BLOCK, seq)
    blk_c = min(DFT_BLOCK, ctx_len)
    dft_x = _dft_operands(blk_x)
    dft_c = _dft_operands(blk_c)
    hy_tab_x = tuple(jnp.asarray(a) for a in _hyena_tables(seq))
    hy_tab_c = tuple(jnp.asarray(a) for a in _hyena_tables(ctx_len))
    rope_x = _rope_tables(seq)
    rope_c = _identity_tables(ctx_len)
    seg = np.arange(LANES) // HEAD_DIM
    seg_mean = jnp.asarray((seg[:, None] == seg[None, :]).astype(np.float32) / HEAD_DIM).astype(BF16)
    n_win = len(POOL_WINDOWS)

    ffn_w = tuple(w.astype(BF16) for w in (ffn_w_gate, ffn_w_up, ffn_w_down))
    w_in_b = w_in.astype(BF16)
    w_out_b = w_out.astype(BF16)

    h, hc = x, ctx
    for l in range(depth):
        last = l == depth - 1
        row_x = lambda s, l=l: l * rows + s
        row_c = lambda s, l=l: l * rows + batch
        w_pool = jnp.zeros((n_win, POOL_GROUP, n_win, POOL_GROUP), F32)
        for gi in range(n_win):
            w_pool = w_pool.at[gi, :, gi, :].set(pool_w[l, gi])
        w_pool = w_pool.reshape(POOL_WIDTH, POOL_WIDTH).astype(BF16)
        qg = jnp.tile(q_norm_g[l], LANES // HEAD_DIM).reshape(1, LANES)
        kg = jnp.tile(k_norm_g[l], LANES // HEAD_DIM).reshape(1, LANES)
        filt_w = _filter_weights(hyena_f_w1[l], hyena_f_b1[l], hyena_f_w2[l], hyena_f_b2[l],
                                 hyena_f_w3[l], hyena_sin_freq[l])

        h, pool_x, hy_x, q_x, k_x, vt_x = _ffn_in_call(
            h, mods, norm_g[l, 0], ffn_w, norm_g[l, 1], w_in_b, rope_x, qg, kg, seg_mean,
            mod_row=row_x, layer=l, half=0)
        hc, pool_c, hy_c, q_c, k_c, vt_c = _ffn_in_call(
            hc, mods, norm_g[l, 0], ffn_w, norm_g[l, 1], w_in_b, rope_c, qg, kg, seg_mean,
            mod_row=row_c, layer=l, half=0)

        chat_x = _filter_call(hy_tab_x, dft_x[0], filt_w, seq=seq, blk=blk_x)
        mix_x = (
            _pool_call(pool_x, w_pool, pool_scale[l]),
            _hyena_call(hy_x, hyena_conv_w[l], hyena_conv_b[l], hyena_bias[l], dft_x[1], dft_x[2],
                        chat_x, blk=blk_x),
            _attn_call(q_x, k_c, vt_c, k_x, vt_x),
        )
        if not last:
            chat_c = _filter_call(hy_tab_c, dft_c[0], filt_w, seq=ctx_len, blk=blk_c)
            mix_c = (
                _pool_call(pool_c, w_pool, pool_scale[l]),
                _hyena_call(hy_c, hyena_conv_w[l], hyena_conv_b[l], hyena_bias[l], dft_c[1], dft_c[2],
                            chat_c, blk=blk_c),
                _attn_call(q_c, k_c, vt_c),
            )
            hc = _ffn_out_call(hc, mods, mix_c, w_out_b, norm_g[l, 2], ffn_w,
                               mod_row=row_c, layer=l, half=1)
        h = _ffn_out_call(h, mods, mix_x, w_out_b, norm_g[l, 2], ffn_w,
                          mod_row=row_x, layer=l, half=1)
    return h
```

```python
import functools
import math

import jax
import jax.numpy as jnp
import numpy as np
from jax import lax
from jax.experimental import pallas as pl
from jax.experimental.pallas import tpu as pltpu

F32 = jnp.float32
BF16 = jnp.bfloat16

GRID_W = 64
EPS = 1e-6
N_MOD = 9
POOL_WIDTH = 256
POOL_WINDOWS = (2, 4, 8, 16)
POOL_GROUP = POOL_WIDTH // len(POOL_WINDOWS)
HYENA_WIDTH = 256
HYENA_ORDER = 2
HYENA_EMB_DIM = 33
HYENA_FILTER_ORDER = 64
HYENA_DECAY_TARGET = 1e-2
HYENA_FAST_DECAY_PCT = 0.3
HYENA_SLOW_DECAY_PCT = 1.5
HEAD_DIM = 64
N_Q_HEADS = 8
N_KV_HEADS = 2
GQA_GROUP = N_Q_HEADS // N_KV_HEADS
ATTN_WIDTH = N_Q_HEADS * HEAD_DIM
KV_WIDTH = N_KV_HEADS * HEAD_DIM
AXIS_DIM = HEAD_DIM // 2
ROPE_THETA = 10000.0
ATTN_SCALE = HEAD_DIM ** -0.5
Q_PRESCALE = ATTN_SCALE * math.log2(math.e)
HYENA_OFF = POOL_WIDTH
Q_OFF = HYENA_OFF + (HYENA_ORDER + 1) * HYENA_WIDTH
K_OFF = Q_OFF + ATTN_WIDTH
V_OFF = K_OFF + KV_WIDTH
IN_WIDTH = V_OFF + KV_WIDTH

LANES = 128
SUBLANES = 8
VMEM_LIMIT_BYTES = 56 * 1024 * 1024

TOKEN_TILE = 512
Q_TILE = 512
Q_SUB = 128
DFT_BLOCK = 512
MOD_COL_TILE = 1152
PAD_ROWS = 16


def _params(n_axes):
    return pltpu.CompilerParams(
        dimension_semantics=("arbitrary",) * n_axes, vmem_limit_bytes=VMEM_LIMIT_BYTES)


def _resident(shape, lead=()):
    index = tuple(lead) + (0,) * len(shape)
    return pl.BlockSpec((None,) * len(lead) + tuple(shape), lambda *_: index,
                        pipeline_mode=pl.Buffered(1))


def _split_bf16(a):
    hi = a.astype(BF16)
    lo = (a - hi.astype(F32)).astype(BF16)
    return hi, lo


def _dot(a, b):
    return jnp.dot(a, b, preferred_element_type=F32)


def _dot3(a, b):
    a_hi, a_lo = _split_bf16(a)
    b_hi, b_lo = _split_bf16(b)
    return _dot(a_hi, b_hi) + _dot(a_lo, b_hi) + _dot(a_hi, b_lo)


def _norm_mod(x, g, shift, scale):
    ms = jnp.mean(x * x, axis=-1, keepdims=True)
    y = x * lax.rsqrt(ms + EPS) * g
    return y * (1.0 + scale) + shift


def _mods_kernel(c_ref, w_ref, b_ref, o_ref):
    rows = c_ref.shape[0]
    cc = c_ref[...]
    act = cc * jax.nn.sigmoid(cc)
    a_hi, a_lo = _split_bf16(act)
    w_hi, w_lo = _split_bf16(w_ref[0])
    r = _dot(jnp.concatenate([a_hi, a_lo], axis=0), w_hi)
    o_ref[0] = r[:rows] + r[rows:] + _dot(a_hi, w_lo) + b_ref[0]


def _mods_call(cvec, w_mod, b_mod):
    depth, d, n = w_mod.shape
    rows = cvec.shape[0]
    tn = MOD_COL_TILE
    return pl.pallas_call(
        _mods_kernel,
        grid=(depth, n // tn),
        in_specs=[
            pl.BlockSpec((rows, d), lambda l, j: (0, 0)),
            pl.BlockSpec((1, d, tn), lambda l, j: (l, 0, j)),
            pl.BlockSpec((1, 1, tn), lambda l, j: (l, 0, j)),
        ],
        out_specs=pl.BlockSpec((1, rows, tn), lambda l, j: (l, 0, j)),
        out_shape=jax.ShapeDtypeStruct((depth, rows, n), F32),
        compiler_params=_params(2),
        name="mods",
    )(cvec, w_mod, b_mod.reshape(depth, 1, n))


FF_CHUNK = 1024


def _ffn_body(x, m, g, wg_ref, wu_ref, wd_ref, mod_off):
    y = _norm_mod(x, g, m[mod_off:mod_off + 1], m[mod_off + 1:mod_off + 2]).astype(BF16)
    d_ff = wg_ref.shape[1]
    acc = jnp.zeros(x.shape, F32)
    for lo in range(0, d_ff, FF_CHUNK):
        hi = min(lo + FF_CHUNK, d_ff)
        a = _dot(y, wg_ref[:, lo:hi])
        u = _dot(y, wu_ref[:, lo:hi])
        mid = (a * jax.nn.sigmoid(a) * u).astype(BF16)
        acc = acc + _dot(mid, wd_ref[lo:hi, :])
    return x + 0.5 * m[mod_off + 2:mod_off + 3] * acc


def _head_norm_rope(x, g, seg_mean, cos, sin_a, sin_b):
    sq_hi, sq_lo = _split_bf16(x * x)
    ms = _dot(sq_hi, seg_mean) + _dot(sq_lo, seg_mean)
    y = x * lax.rsqrt(ms + EPS) * g
    return (y * cos + pltpu.roll(y, AXIS_DIM // 2, 1) * sin_a
            + pltpu.roll(y, LANES - AXIS_DIM // 2, 1) * sin_b)


def _ffn_in_kernel(x_ref, mod_ref, g0_ref, wg_ref, wu_ref, wd_ref, g1_ref, w_ref,
                   cos_ref, sa_ref, sb_ref, qg_ref, kg_ref, sm_ref,
                   h_ref, pool_ref, hy_ref, q_ref, k_ref, v_ref):
    m = mod_ref[0]
    h = _ffn_body(x_ref[0], m, g0_ref[...], wg_ref, wu_ref, wd_ref, 0)
    h_ref[0] = h
    y = _norm_mod(h, g1_ref[...], m[3:4], m[4:5]).astype(BF16)
    p = _dot(y, w_ref[...])
    pool_ref[0] = p[:, :HYENA_OFF]
    hy_ref[0] = p[:, HYENA_OFF:Q_OFF]
    cos, sa, sb, sm = cos_ref[...], sa_ref[...], sb_ref[...], sm_ref[...]
    qs = []
    for j in range(ATTN_WIDTH // LANES):
        qj = p[:, Q_OFF + j * LANES:Q_OFF + (j + 1) * LANES]
        qs.append(_head_norm_rope(qj, qg_ref[...], sm, cos, sa, sb) * Q_PRESCALE)
    q_ref[0] = jnp.concatenate(qs, axis=1).astype(BF16)
    k_ref[0] = _head_norm_rope(p[:, K_OFF:V_OFF], kg_ref[...], sm, cos, sa, sb).astype(BF16)
    v_ref[0] = p[:, V_OFF:].astype(BF16)


def _ffn_in_call(h, mods, g0, ffn_w, g1, w_in, tables, qg, kg, seg_mean, *, mod_row, layer, half):
    n_seq, lq, d = h.shape
    tm = min(TOKEN_TILE, lq)
    wg, wu, wd = ffn_w
    d_ff = wg.shape[-1]
    cos, sa, sb = tables
    tab_spec = pl.BlockSpec((tm, LANES), lambda s, i: (i, 0))
    tok = lambda w: pl.BlockSpec((1, tm, w), lambda s, i: (s, i, 0))
    shp = lambda w, dt: jax.ShapeDtypeStruct((n_seq, lq, w), dt)
    return pl.pallas_call(
        _ffn_in_kernel,
        grid=(n_seq, lq // tm),
        in_specs=[
            tok(d),
            pl.BlockSpec((1, N_MOD, d), lambda s, i: (mod_row(s), 0, 0)),
            _resident((1, d)),
            _resident((d, d_ff), (layer, half)), _resident((d, d_ff), (layer, half)),
            _resident((d_ff, d), (layer, half)),
            _resident((1, d)),
            _resident((d, IN_WIDTH), (layer,)),
            tab_spec, tab_spec, tab_spec,
            _resident((1, LANES)), _resident((1, LANES)), _resident((LANES, LANES)),
        ],
        out_specs=[tok(d), tok(POOL_WIDTH), tok(Q_OFF - HYENA_OFF), tok(ATTN_WIDTH),
                   tok(KV_WIDTH), tok(KV_WIDTH)],
        out_shape=[shp(d, F32), shp(POOL_WIDTH, F32), shp(Q_OFF - HYENA_OFF, F32),
                   shp(ATTN_WIDTH, BF16), shp(KV_WIDTH, BF16), shp(KV_WIDTH, BF16)],
        compiler_params=_params(2),
        name="ffn_in",
    )(h, mods, g0.reshape(1, d), wg, wu, wd, g1.reshape(1, d), w_in, cos, sa, sb, qg, kg, seg_mean)


def _ffn_out_kernel(h_ref, mod_ref, a_ref, b_ref, c_ref, w_ref, g_ref, wg_ref, wu_ref, wd_ref, o_ref):
    m = mod_ref[0]
    wa = a_ref.shape[2]
    wb = b_ref.shape[2]
    mix = (_dot(a_ref[0], w_ref[0:wa, :]) + _dot(b_ref[0], w_ref[wa:wa + wb, :])
           + _dot(c_ref[0], w_ref[wa + wb:, :]))
    h = h_ref[0] + m[5:6] * mix
    o_ref[0] = _ffn_body(h, m, g_ref[...], wg_ref, wu_ref, wd_ref, 6)


def _ffn_out_call(h, mods, mix, w_out, g, ffn_w, *, mod_row, layer, half):
    n_seq, lq, d = h.shape
    tm = min(TOKEN_TILE, lq)
    wg, wu, wd = ffn_w
    d_ff = wg.shape[-1]
    tok = lambda w: pl.BlockSpec((1, tm, w), lambda s, i: (s, i, 0))
    return pl.pallas_call(
        _ffn_out_kernel,
        grid=(n_seq, lq // tm),
        in_specs=[
            tok(d),
            pl.BlockSpec((1, N_MOD, d), lambda s, i: (mod_row(s), 0, 0)),
            tok(mix[0].shape[2]), tok(mix[1].shape[2]), tok(mix[2].shape[2]),
            _resident(w_out.shape[1:], (layer,)),
            _resident((1, d)),
            _resident((d, d_ff), (layer, half)), _resident((d, d_ff), (layer, half)),
            _resident((d_ff, d), (layer, half)),
        ],
        out_specs=tok(d),
        out_shape=jax.ShapeDtypeStruct((n_seq, lq, d), F32),
        compiler_params=_params(2),
        name="ffn_out",
    )(h, mods, *mix, w_out, g.reshape(1, d), wg, wu, wd)


def _pool_kernel(u_ref, w_ref, sc_ref, o_ref, p_ref, a2_ref, a4_ref, a8_ref, *, seq):
    u = u_ref[0]
    width = u.shape[1]
    pad = PAD_ROWS
    span = seq + pad
    zero_pad = jnp.zeros((pad, width), F32)
    zero_edge = jnp.zeros((SUBLANES, width), F32)
    p_ref[0:pad] = zero_pad
    p_ref[pad + seq:2 * pad + seq] = zero_pad
    p_ref[pad:pad + seq] = u
    for ref in (a2_ref, a4_ref, a8_ref):
        ref[0:SUBLANES] = zero_edge
        ref[SUBLANES + span:2 * SUBLANES + span] = zero_edge
    a2_ref[SUBLANES:SUBLANES + span] = p_ref[7:7 + span] + p_ref[8:8 + span]
    a4_ref[SUBLANES:SUBLANES + span] = a2_ref[7:7 + span] + a2_ref[9:9 + span]
    a8_ref[SUBLANES:SUBLANES + span] = a4_ref[6:6 + span] + a4_ref[10:10 + span]
    s16 = a8_ref[pad - 4:pad - 4 + seq] + a8_ref[pad + 4:pad + 4 + seq]
    s2 = a2_ref[pad:pad + seq]
    s4 = a4_ref[pad:pad + seq]
    s8 = a8_ref[pad:pad + seq]
    lane = lax.broadcasted_iota(jnp.int32, (seq, width), 1)
    t = lax.broadcasted_iota(jnp.int32, (seq, width), 0)
    g0, g1, g2 = lane < POOL_GROUP, lane < 2 * POOL_GROUP, lane < 3 * POOL_GROUP
    half = jnp.where(g0, POOL_WINDOWS[0] // 2,
                     jnp.where(g1, POOL_WINDOWS[1] // 2,
                               jnp.where(g2, POOL_WINDOWS[2] // 2, POOL_WINDOWS[3] // 2)))
    count = (jnp.minimum(t + half, seq) - jnp.maximum(t - half, 0)).astype(F32)
    wsum = jnp.where(g0, s2, jnp.where(g1, s4, jnp.where(g2, s8, s16)))
    pooled = wsum / count - u
    o_ref[0] = (_dot(pooled.astype(BF16), w_ref[...]) * sc_ref[...]).astype(BF16)


def _pool_call(u, w_bd, scale):
    n_seq, lq, width = u.shape
    buf = pltpu.VMEM((lq + 2 * PAD_ROWS, width), F32)
    return pl.pallas_call(
        functools.partial(_pool_kernel, seq=lq),
        grid=(n_seq,),
        in_specs=[
            pl.BlockSpec((1, lq, width), lambda s: (s, 0, 0)),
            _resident((width, width)),
            _resident((1, width)),
        ],
        out_specs=pl.BlockSpec((1, lq, width), lambda s: (s, 0, 0)),
        out_shape=jax.ShapeDtypeStruct((n_seq, lq, width), BF16),
        scratch_shapes=[buf, buf, buf, buf],
        compiler_params=_params(1),
        name="pool",
    )(u, w_bd, scale.reshape(1, width))


def _dft_tables(b):
    n = 2 * b
    f = np.arange(b, dtype=np.int64)[:, None]
    s = np.arange(b, dtype=np.int64)[None, :]

    def packed(expo):
        ang = 2.0 * np.pi * ((f * expo) % n) / n
        w = np.concatenate([np.cos(ang), -np.sin(ang)], axis=0)
        w[b] = np.where((expo[0] % 2) == 0, 1.0, -1.0)
        return w

    fwd = packed(s)
    rev = packed(b - 1 - s)
    t = np.arange(b, dtype=np.int64)[:, None]
    fr = np.arange(b, dtype=np.int64)[None, :]
    ang = 2.0 * np.pi * ((fr * t) % n) / n
    inv = np.concatenate([2.0 * np.cos(ang), -2.0 * np.sin(ang)], axis=1) / n
    inv[:, 0] = 1.0 / n
    inv[:, b] = np.where((t[:, 0] % 2) == 0, 1.0, -1.0) / n
    return fwd.astype(np.float32), rev.astype(np.float32), inv.astype(np.float32)


def _hyena_tables(seq):
    t = np.linspace(0.0, 1.0, seq)[:, None]
    bands = (HYENA_EMB_DIM - 1) // 2
    w_ang = 2.0 * np.pi * np.arange(seq) / seq
    freqs = np.linspace(1e-4, bands - 1, bands)
    ang = w_ang[:, None] * freqs[None, :]
    z = np.concatenate([t, np.cos(ang), -np.sin(ang)], axis=-1)
    max_decay = math.log(HYENA_DECAY_TARGET) / HYENA_FAST_DECAY_PCT
    min_decay = math.log(HYENA_DECAY_TARGET) / HYENA_SLOW_DECAY_PCT
    deltas = np.linspace(min_decay, max_decay, HYENA_WIDTH)
    decay = np.exp(-t * np.abs(deltas))
    zz = np.zeros((seq, 2 * LANES))
    zz[:, :HYENA_EMB_DIM] = z
    zz[:-1, LANES:LANES + HYENA_EMB_DIM] = z[1:]
    decay_shift = np.zeros_like(decay)
    decay_shift[:-1] = decay[1:]
    return tuple(a.astype(np.float32) for a in (zz, decay, decay_shift))


def _dot_table(w_hi, w_lo, x):
    x_hi, x_lo = _split_bf16(x)
    return _dot(w_hi, x_hi) + _dot(w_hi, x_lo) + _dot(w_lo, x_hi)


def _filter_kernel(zz_ref, dec_ref, decs_ref, w1_ref, b1_ref, w2_ref, b2_ref, w3_ref, sf_ref,
                   fh_ref, fl_ref, rh_ref, rl_ref, o_ref, filt_ref, *, blk, nblk):
    order = pl.program_id(0)

    @pl.when(order == 0)
    def _():
        sf = sf_ref[...]
        h1 = jnp.sin(sf[0:1] * (_dot3(zz_ref[...], w1_ref[...]) + b1_ref[...]))
        h2 = jnp.sin(sf[1:2] * (_dot3(h1, w2_ref[...]) + b2_ref[...]))
        taps = _dot3(h2, w3_ref[...])
        for o in range(HYENA_ORDER):
            fwd = taps[:, o * HYENA_WIDTH:(o + 1) * HYENA_WIDTH] * dec_ref[...]
            bwd = (taps[:, (HYENA_ORDER + o) * HYENA_WIDTH:(HYENA_ORDER + o + 1) * HYENA_WIDTH]
                   * decs_ref[...])
            norm = (jnp.sum(jnp.abs(fwd), axis=0, keepdims=True)
                    + jnp.sum(jnp.abs(bwd), axis=0, keepdims=True))
            filt_ref[o] = fwd / norm
            filt_ref[HYENA_ORDER + o] = bwd / norm

    fwd = filt_ref[order]
    bwd = filt_ref[HYENA_ORDER + order]
    fh, fl, rh, rl = fh_ref[...], fl_ref[...], rh_ref[...], rl_ref[...]
    spec = {}
    for d in range(nblk):
        spec[d] = _dot_table(fh, fl, fwd[d * blk:(d + 1) * blk])
        spec[-(d + 1)] = _dot_table(rh, rl, bwd[d * blk:(d + 1) * blk])
    row = lax.broadcasted_iota(jnp.int32, (2 * blk, HYENA_WIDTH), 0)
    sign = (1 - 2 * (row & 1)).astype(F32)
    for d in range(-(nblk - 1), nblk):
        o_ref[0, d + nblk - 1] = spec[d] + sign * spec[d - 1]


def _filter_weights(w1, b1, w2, b2, w3, sf):
    fo = HYENA_FILTER_ORDER

    def block_diag(a, b):
        top = jnp.concatenate([a, jnp.zeros((a.shape[0], b.shape[1]), F32)], axis=1)
        bot = jnp.concatenate([jnp.zeros((b.shape[0], a.shape[1]), F32), b], axis=1)
        return jnp.concatenate([top, bot], axis=0)

    w1p = jnp.zeros((LANES, fo), F32).at[:HYENA_EMB_DIM].set(w1)
    w3r = w3.reshape(fo, HYENA_ORDER, 2, HYENA_WIDTH)
    w3f = w3r[:, :, 0].reshape(fo, HYENA_ORDER * HYENA_WIDTH)
    w3b = w3r[:, :, 1].reshape(fo, HYENA_ORDER * HYENA_WIDTH)
    return (block_diag(w1p, w1p), jnp.tile(b1, 2).reshape(1, 2 * fo),
            block_diag(w2, w2), jnp.tile(b2, 2).reshape(1, 2 * fo),
            block_diag(w3f, w3b), jnp.tile(sf, (1, 2)))


def _filter_call(tables, dft, weights, *, seq, blk):
    nblk = seq // blk
    zz, dec, decs = tables
    fh, fl, rh, rl = dft
    fo2 = 2 * HYENA_FILTER_ORDER
    n_taps = 2 * HYENA_ORDER * HYENA_WIDTH
    return pl.pallas_call(
        functools.partial(_filter_kernel, blk=blk, nblk=nblk),
        grid=(HYENA_ORDER,),
        in_specs=[
            _resident((seq, 2 * LANES)),
            _resident((seq, HYENA_WIDTH)), _resident((seq, HYENA_WIDTH)),
            _resident((2 * LANES, fo2)), _resident((1, fo2)),
            _resident((fo2, fo2)), _resident((1, fo2)),
            _resident((fo2, n_taps)),
            _resident((2, fo2)),
            _resident((2 * blk, blk)), _resident((2 * blk, blk)),
            _resident((2 * blk, blk)), _resident((2 * blk, blk)),
        ],
        out_specs=pl.BlockSpec((1, 2 * nblk - 1, 2 * blk, HYENA_WIDTH), lambda o: (o, 0, 0, 0)),
        out_shape=jax.ShapeDtypeStruct((HYENA_ORDER, 2 * nblk - 1, 2 * blk, HYENA_WIDTH), F32),
        scratch_shapes=[pltpu.VMEM((2 * HYENA_ORDER, seq, HYENA_WIDTH), F32)],
        compiler_params=_params(1),
        name="hyena_filter",
    )(zz, dec, decs, *weights, fh, fl, rh, rl)


def _short_conv3(p, cw, cb):
    seq, width = p.shape
    zc = pltpu.roll(p, 1, 0) * cw[0:1] + p * cw[1:2] + pltpu.roll(p, seq - 1, 0) * cw[2:3] + cb
    row = lax.broadcasted_iota(jnp.int32, (SUBLANES, width), 0)
    head = zc[:SUBLANES] - jnp.where(row == 0, p[seq - 1:seq] * cw[0:1], 0.0)
    tail = zc[seq - SUBLANES:] - jnp.where(row == SUBLANES - 1, p[0:1] * cw[2:3], 0.0)
    return jnp.concatenate([head, zc[SUBLANES:seq - SUBLANES], tail], axis=0)


def _hyena_kernel(p_ref, cw_ref, cb_ref, hb_ref, wf_ref, wi_ref, ch_ref, o_ref, *, seq, blk):
    nblk = seq // blk
    zc = _short_conv3(p_ref[0], cw_ref[...], cb_ref[...])
    y = zc[:, :HYENA_WIDTH]
    wf = wf_ref[...]
    wi = wi_ref[...]
    hb = hb_ref[...]
    first = lax.broadcasted_iota(jnp.int32, (blk, HYENA_WIDTH), 0) == 0
    for o in range(HYENA_ORDER):
        gate = zc[:, (o + 1) * HYENA_WIDTH:(o + 2) * HYENA_WIDTH]
        spec = [_dot(wf, y[j * blk:(j + 1) * blk].astype(BF16)) for j in range(nblk)]
        outs = []
        for i in range(nblk):
            acc_r = jnp.zeros((blk, HYENA_WIDTH), F32)
            acc_i = jnp.zeros((blk, HYENA_WIDTH), F32)
            for j in range(nblk):
                cm = ch_ref[o, i - j + nblk - 1]
                ur, ui = spec[j][:blk], spec[j][blk:]
                cr, ci = cm[:blk], cm[blk:]
                ii = ui * ci
                acc_r = acc_r + ur * cr - jnp.where(first, 0.0, ii)
                acc_i = acc_i + jnp.where(first, ii, ur * ci + ui * cr)
            prod = jnp.concatenate([acc_r, acc_i], axis=0).astype(BF16)
            outs.append(_dot(wi, prod))
        conv = outs[0] if nblk == 1 else jnp.concatenate(outs, axis=0)
        y = gate * (conv + hb[o:o + 1] * y)
    o_ref[0] = y.astype(BF16)


def _hyena_call(p, conv_w, conv_b, hy_bias, wf, wi, chat, *, blk):
    n_seq, lq, width = p.shape
    nblk = lq // blk
    return pl.pallas_call(
        functools.partial(_hyena_kernel, seq=lq, blk=blk),
        grid=(n_seq,),
        in_specs=[
            pl.BlockSpec((1, lq, width), lambda s: (s, 0, 0)),
            _resident((3, width)),
            _resident((1, width)),
            _resident((HYENA_ORDER, HYENA_WIDTH)),
            _resident((2 * blk, blk)),
            _resident((blk, 2 * blk)),
            _resident((HYENA_ORDER, 2 * nblk - 1, 2 * blk, HYENA_WIDTH)),
        ],
        out_specs=pl.BlockSpec((1, lq, HYENA_WIDTH), lambda s: (s, 0, 0)),
        out_shape=jax.ShapeDtypeStruct((n_seq, lq, HYENA_WIDTH), BF16),
        compiler_params=_params(1),
        name="hyena",
    )(p, conv_w, conv_b.reshape(1, width), hy_bias, wf, wi, chat)


def _qk(q, k):
    return lax.dot_general(q, k, (((1,), (1,)), ((), ())), preferred_element_type=F32)


def _attn_kernel(*refs, with_latent):
    if with_latent:
        q_ref, k_ref, v_ref, kc_ref, vc_ref, o_ref = refs
    else:
        q_ref, kc_ref, vc_ref, o_ref = refs
    q = q_ref[0]
    tq = q.shape[0]
    sub = min(Q_SUB, tq)
    pieces = [[None] * N_Q_HEADS for _ in range(tq // sub)]
    for kh in range(N_KV_HEADS):
        cols = slice(kh * HEAD_DIM, (kh + 1) * HEAD_DIM)
        k_c, v_c = kc_ref[0][:, cols], vc_ref[0][:, cols]
        if with_latent:
            k_l, v_l = k_ref[0][:, cols], v_ref[0][:, cols]
        for r in range(tq // sub):
            rows = slice(r * sub, (r + 1) * sub)
            qs = jnp.concatenate(
                [q[rows, (kh * GQA_GROUP + g) * HEAD_DIM:(kh * GQA_GROUP + g + 1) * HEAD_DIM]
                 for g in range(GQA_GROUP)], axis=0)
            s_c = _qk(qs, k_c)
            m = jnp.max(s_c, axis=-1, keepdims=True)
            if with_latent:
                s_l = _qk(qs, k_l)
                m = jnp.maximum(m, jnp.max(s_l, axis=-1, keepdims=True))
            p_c = jnp.exp2(s_c - m)
            den = jnp.sum(p_c, axis=-1, keepdims=True)
            o = _dot(p_c.astype(BF16), v_c)
            if with_latent:
                p_l = jnp.exp2(s_l - m)
                den = den + jnp.sum(p_l, axis=-1, keepdims=True)
                o = o + _dot(p_l.astype(BF16), v_l)
            o = o / den
            for g in range(GQA_GROUP):
                pieces[r][kh * GQA_GROUP + g] = o[g * sub:(g + 1) * sub]
    o_ref[0] = jnp.concatenate(
        [jnp.concatenate(row, axis=1) for row in pieces], axis=0).astype(BF16)


def _attn_call(q, kc, vc, k=None, v=None):
    n_seq, lq, _ = q.shape
    tq = min(Q_TILE, lq)
    ctx_len = kc.shape[1]
    with_latent = k is not None
    q_spec = pl.BlockSpec((1, tq, ATTN_WIDTH), lambda s, i: (s, i, 0))
    kv_spec = lambda n: pl.BlockSpec((1, n, KV_WIDTH), lambda s, i: (s, 0, 0))
    if with_latent:
        in_specs = [q_spec, kv_spec(lq), kv_spec(lq), kv_spec(ctx_len), kv_spec(ctx_len)]
        args = (q, k, v, kc, vc)
    else:
        in_specs = [q_spec, kv_spec(ctx_len), kv_spec(ctx_len)]
        args = (q, kc, vc)
    return pl.pallas_call(
        functools.partial(_attn_kernel, with_latent=with_latent),
        grid=(n_seq, lq // tq),
        in_specs=in_specs,
        out_specs=q_spec,
        out_shape=jax.ShapeDtypeStruct((n_seq, lq, ATTN_WIDTH), BF16),
        compiler_params=_params(2),
        name="attn" if with_latent else "attn_ctx",
    )(*args)


def _rope_tables(seq):
    rows = seq // GRID_W
    row = np.repeat(np.arange(rows), GRID_W).astype(np.float64)
    col = np.tile(np.arange(GRID_W), rows).astype(np.float64)
    inv_freq = 1.0 / (ROPE_THETA ** (np.arange(0, AXIS_DIM, 2, dtype=np.float64) / AXIS_DIM))
    ang_r = row[:, None] * inv_freq
    ang_c = col[:, None] * inv_freq
    ang = np.concatenate([ang_r, ang_r, ang_c, ang_c], axis=-1)
    cos, sin = np.cos(ang), np.sin(ang)
    upper = (np.arange(HEAD_DIM) % AXIS_DIM) >= AXIS_DIM // 2
    sin_a = np.where(upper, sin, 0.0)
    sin_b = np.where(upper, 0.0, -sin)
    rep = LANES // HEAD_DIM
    return tuple(jnp.asarray(np.tile(a, (1, rep)).astype(np.float32)) for a in (cos, sin_a, sin_b))


def _identity_tables(seq):
    one = jnp.ones((seq, LANES), F32)
    zero = jnp.zeros((seq, LANES), F32)
    return one, zero, zero


def _dft_operands(blk):
    fwd, rev, inv = (jnp.asarray(a) for a in _dft_tables(blk))
    fh, fl = _split_bf16(fwd)
    rh, rl = _split_bf16(rev)
    return (fh, fl, rh, rl), fh, inv.astype(BF16)


def kernel(x, c, ctx, c_ctx, norm_g, w_mod, b_mod, ffn_w_gate, ffn_w_up, ffn_w_down, w_in, w_out, pool_w, pool_scale, hyena_conv_w, hyena_conv_b, hyena_f_w1, hyena_f_b1, hyena_f_w2, hyena_f_b2, hyena_f_w3, hyena_sin_freq, hyena_bias, q_norm_g, k_norm_g):
    batch, seq, d = x.shape
    ctx_len = ctx.shape[1]
    depth = w_mod.shape[0]

    rows = -(-(batch + 1) // SUBLANES) * SUBLANES
    cvec = jnp.zeros((rows, d), F32).at[:batch].set(c).at[batch].set(c_ctx)
    mods = _mods_call(cvec, w_mod, b_mod).reshape(depth * rows, N_MOD, d)

    blk_x = min(DFT_BLOCK, seq)
    blk_c = min(DFT_BLOCK, ctx_len)
    dft_x = _dft_operands(blk_x)
    dft_c = _dft_operands(blk_c)
    hy_tab_x = tuple(jnp.asarray(a) for a in _hyena_tables(seq))
    hy_tab_c = tuple(jnp.asarray(a) for a in _hyena_tables(ctx_len))
    rope_x = _rope_tables(seq)
    rope_c = _identity_tables(ctx_len)
    seg = np.arange(LANES) // HEAD_DIM
    seg_mean = jnp.asarray((seg[:, None] == seg[None, :]).astype(np.float32) / HEAD_DIM).astype(BF16)
    n_win = len(POOL_WINDOWS)

    ffn_w = tuple(w.astype(BF16) for w in (ffn_w_gate, ffn_w_up, ffn_w_down))
    w_in_b = w_in.astype(BF16)
    w_out_b = w_out.astype(BF16)

    h, hc = x, ctx
    for l in range(depth):
        last = l == depth - 1
        row_x = lambda s, l=l: l * rows + s
        row_c = lambda s, l=l: l * rows + batch
        w_pool = jnp.zeros((n_win, POOL_GROUP, n_win, POOL_GROUP), F32)
        for gi in range(n_win):
            w_pool = w_pool.at[gi, :, gi, :].set(pool_w[l, gi])
        w_pool = w_pool.reshape(POOL_WIDTH, POOL_WIDTH).astype(BF16)
        qg = jnp.tile(q_norm_g[l], LANES // HEAD_DIM).reshape(1, LANES)
        kg = jnp.tile(k_norm_g[l], LANES // HEAD_DIM).reshape(1, LANES)
        filt_w = _filter_weights(hyena_f_w1[l], hyena_f_b1[l], hyena_f_w2[l], hyena_f_b2[l],
                                 hyena_f_w3[l], hyena_sin_freq[l])

        h, pool_x, hy_x, q_x, k_x, v_x = _ffn_in_call(
            h, mods, norm_g[l, 0], ffn_w, norm_g[l, 1], w_in_b, rope_x, qg, kg, seg_mean,
            mod_row=row_x, layer=l, half=0)
        hc, pool_c, hy_c, q_c, k_c, v_c = _ffn_in_call(
            hc, mods, norm_g[l, 0], ffn_w, norm_g[l, 1], w_in_b, rope_c, qg, kg, seg_mean,
            mod_row=row_c, layer=l, half=0)

        chat_x = _filter_call(hy_tab_x, dft_x[0], filt_w, seq=seq, blk=blk_x)
        mix_x = (
            _pool_call(pool_x, w_pool, pool_scale[l]),
            _hyena_call(hy_x, hyena_conv_w[l], hyena_conv_b[l], hyena_bias[l], dft_x[1], dft_x[2],
                        chat_x, blk=blk_x),
            _attn_call(q_x, k_c, v_c, k_x, v_x),
        )
        if not last:
            chat_c = _filter_call(hy_tab_c, dft_c[0], filt_w, seq=ctx_len, blk=blk_c)
            mix_c = (
                _pool_call(pool_c, w_pool, pool_scale[l]),
                _hyena_call(hy_c, hyena_conv_w[l], hyena_conv_b[l], hyena_bias[l], dft_c[1], dft_c[2],
                            chat_c, blk=blk_c),
                _attn_call(q_c, k_c, v_c),
            )
            hc = _ffn_out_call(hc, mods, mix_c, w_out_b, norm_g[l, 2], ffn_w,
                               mod_row=row_c, layer=l, half=1)
        h = _ffn_out_call(h, mods, mix_x, w_out_b, norm_g[l, 2], ffn_w,
                          mod_row=row_x, layer=l, half=1)
    return h
```

```python
import functools
import math

import jax
import jax.numpy as jnp
import numpy as np
from jax import lax
from jax.experimental import pallas as pl
from jax.experimental.pallas import tpu as pltpu

F32 = jnp.float32
BF16 = jnp.bfloat16

GRID_W = 64
EPS = 1e-6
N_MOD = 9
POOL_WIDTH = 256
POOL_WINDOWS = (2, 4, 8, 16)
POOL_GROUP = POOL_WIDTH // len(POOL_WINDOWS)
HYENA_WIDTH = 256
HYENA_ORDER = 2
HYENA_EMB_DIM = 33
HYENA_FILTER_ORDER = 64
HYENA_DECAY_TARGET = 1e-2
HYENA_FAST_DECAY_PCT = 0.3
HYENA_SLOW_DECAY_PCT = 1.5
HEAD_DIM = 64
N_Q_HEADS = 8
N_KV_HEADS = 2
GQA_GROUP = N_Q_HEADS // N_KV_HEADS
ATTN_WIDTH = N_Q_HEADS * HEAD_DIM
KV_WIDTH = N_KV_HEADS * HEAD_DIM
AXIS_DIM = HEAD_DIM // 2
ROPE_THETA = 10000.0
ATTN_SCALE = HEAD_DIM ** -0.5
Q_PRESCALE = ATTN_SCALE * math.log2(math.e)
HYENA_OFF = POOL_WIDTH
Q_OFF = HYENA_OFF + (HYENA_ORDER + 1) * HYENA_WIDTH
K_OFF = Q_OFF + ATTN_WIDTH
V_OFF = K_OFF + KV_WIDTH
IN_WIDTH = V_OFF + KV_WIDTH

LANES = 128
SUBLANES = 8
VMEM_LIMIT_BYTES = 56 * 1024 * 1024

TOKEN_TILE = 512
Q_TILE = 512
Q_SUB = 128
DFT_BLOCK = 512
MOD_COL_TILE = 1152
PAD_ROWS = 16


def _params(n_axes):
    return pltpu.CompilerParams(
        dimension_semantics=("arbitrary",) * n_axes, vmem_limit_bytes=VMEM_LIMIT_BYTES)


def _resident(shape, lead=()):
    index = tuple(lead) + (0,) * len(shape)
    return pl.BlockSpec((None,) * len(lead) + tuple(shape), lambda *_: index,
                        pipeline_mode=pl.Buffered(1))


def _split_bf16(a):
    hi = a.astype(BF16)
    lo = (a - hi.astype(F32)).astype(BF16)
    return hi, lo


def _dot(a, b):
    return jnp.dot(a, b, preferred_element_type=F32)


def _dot3(a, b):
    a_hi, a_lo = _split_bf16(a)
    b_hi, b_lo = _split_bf16(b)
    return _dot(a_hi, b_hi) + _dot(a_lo, b_hi) + _dot(a_hi, b_lo)


def _norm_mod(x, g, shift, scale):
    ms = jnp.mean(x * x, axis=-1, keepdims=True)
    y = x * lax.rsqrt(ms + EPS) * g
    return y * (1.0 + scale) + shift


def _mods_kernel(c_ref, w_ref, b_ref, o_ref):
    rows = c_ref.shape[0]
    cc = c_ref[...]
    act = cc * jax.nn.sigmoid(cc)
    a_hi, a_lo = _split_bf16(act)
    w_hi, w_lo = _split_bf16(w_ref[0])
    r = _dot(jnp.concatenate([a_hi, a_lo], axis=0), w_hi)
    o_ref[0] = r[:rows] + r[rows:] + _dot(a_hi, w_lo) + b_ref[0]


def _mods_call(cvec, w_mod, b_mod):
    depth, d, n = w_mod.shape
    rows = cvec.shape[0]
    tn = MOD_COL_TILE
    return pl.pallas_call(
        _mods_kernel,
        grid=(depth, n // tn),
        in_specs=[
            pl.BlockSpec((rows, d), lambda l, j: (0, 0)),
            pl.BlockSpec((1, d, tn), lambda l, j: (l, 0, j)),
            pl.BlockSpec((1, 1, tn), lambda l, j: (l, 0, j)),
        ],
        out_specs=pl.BlockSpec((1, rows, tn), lambda l, j: (l, 0, j)),
        out_shape=jax.ShapeDtypeStruct((depth, rows, n), F32),
        compiler_params=_params(2),
        name="mods",
    )(cvec, w_mod, b_mod.reshape(depth, 1, n))


FF_CHUNK = 1024


def _ffn_body(x, m, g, wg_ref, wu_ref, wd_ref, mod_off):
    y = _norm_mod(x, g, m[mod_off:mod_off + 1], m[mod_off + 1:mod_off + 2]).astype(BF16)
    d_ff = wg_ref.shape[1]
    acc = jnp.zeros(x.shape, F32)
    for lo in range(0, d_ff, FF_CHUNK):
        hi = min(lo + FF_CHUNK, d_ff)
        a = _dot(y, wg_ref[:, lo:hi])
        u = _dot(y, wu_ref[:, lo:hi])
        mid = (a * jax.nn.sigmoid(a) * u).astype(BF16)
        acc = acc + _dot(mid, wd_ref[lo:hi, :])
    return x + 0.5 * m[mod_off + 2:mod_off + 3] * acc


def _head_norm_rope(x, g, seg_mean, cos, sin_a, sin_b):
    sq_hi, sq_lo = _split_bf16(x * x)
    ms = _dot(sq_hi, seg_mean) + _dot(sq_lo, seg_mean)
    y = x * lax.rsqrt(ms + EPS) * g
    return (y * cos + pltpu.roll(y, AXIS_DIM // 2, 1) * sin_a
            + pltpu.roll(y, LANES - AXIS_DIM // 2, 1) * sin_b)


def _ffn_in_kernel(x_ref, mod_ref, g0_ref, wg_ref, wu_ref, wd_ref, g1_ref, w_ref,
                   cos_ref, sa_ref, sb_ref, qg_ref, kg_ref, sm_ref,
                   h_ref, pool_ref, hy_ref, q_ref, k_ref, v_ref):
    m = mod_ref[0]
    h = _ffn_body(x_ref[0], m, g0_ref[...], wg_ref, wu_ref, wd_ref, 0)
    h_ref[0] = h
    y = _norm_mod(h, g1_ref[...], m[3:4], m[4:5]).astype(BF16)
    p = _dot(y, w_ref[...])
    pool_ref[0] = p[:, :HYENA_OFF]
    hy_ref[0] = p[:, HYENA_OFF:Q_OFF]
    cos, sa, sb, sm = cos_ref[...], sa_ref[...], sb_ref[...], sm_ref[...]
    qs = []
    for j in range(ATTN_WIDTH // LANES):
        qj = p[:, Q_OFF + j * LANES:Q_OFF + (j + 1) * LANES]
        qs.append(_head_norm_rope(qj, qg_ref[...], sm, cos, sa, sb) * Q_PRESCALE)
    q_ref[0] = jnp.concatenate(qs, axis=1).astype(BF16)
    k_ref[0] = _head_norm_rope(p[:, K_OFF:V_OFF], kg_ref[...], sm, cos, sa, sb).astype(BF16)
    v_ref[0] = p[:, V_OFF:].astype(BF16)


def _ffn_in_call(h, mods, g0, ffn_w, g1, w_in, tables, qg, kg, seg_mean, *, mod_row, layer, half):
    n_seq, lq, d = h.shape
    tm = min(TOKEN_TILE, lq)
    wg, wu, wd = ffn_w
    d_ff = wg.shape[-1]
    cos, sa, sb = tables
    tab_spec = pl.BlockSpec((tm, LANES), lambda s, i: (i, 0))
    tok = lambda w: pl.BlockSpec((1, tm, w), lambda s, i: (s, i, 0))
    shp = lambda w, dt: jax.ShapeDtypeStruct((n_seq, lq, w), dt)
    return pl.pallas_call(
        _ffn_in_kernel,
        grid=(n_seq, lq // tm),
        in_specs=[
            tok(d),
            pl.BlockSpec((1, N_MOD, d), lambda s, i: (mod_row(s), 0, 0)),
            _resident((1, d)),
            _resident((d, d_ff), (layer, half)), _resident((d, d_ff), (layer, half)),
            _resident((d_ff, d), (layer, half)),
            _resident((1, d)),
            _resident((d, IN_WIDTH), (layer,)),
            tab_spec, tab_spec, tab_spec,
            _resident((1, LANES)), _resident((1, LANES)), _resident((LANES, LANES)),
        ],
        out_specs=[tok(d), tok(POOL_WIDTH), tok(Q_OFF - HYENA_OFF), tok(ATTN_WIDTH),
                   tok(KV_WIDTH), tok(KV_WIDTH)],
        out_shape=[shp(d, F32), shp(POOL_WIDTH, F32), shp(Q_OFF - HYENA_OFF, F32),
                   shp(ATTN_WIDTH, BF16), shp(KV_WIDTH, BF16), shp(KV_WIDTH, BF16)],
        compiler_params=_params(2),
        name="ffn_in",
    )(h, mods, g0.reshape(1, d), wg, wu, wd, g1.reshape(1, d), w_in, cos, sa, sb, qg, kg, seg_mean)


def _ffn_out_kernel(h_ref, mod_ref, a_ref, b_ref, c_ref, w_ref, g_ref, wg_ref, wu_ref, wd_ref, o_ref):
    m = mod_ref[0]
    wa = a_ref.shape[2]
    wb = b_ref.shape[2]
    mix = (_dot(a_ref[0], w_ref[0:wa, :]) + _dot(b_ref[0], w_ref[wa:wa + wb, :])
           + _dot(c_ref[0], w_ref[wa + wb:, :]))
    h = h_ref[0] + m[5:6] * mix
    o_ref[0] = _ffn_body(h, m, g_ref[...], wg_ref, wu_ref, wd_ref, 6)


def _ffn_out_call(h, mods, mix, w_out, g, ffn_w, *, mod_row, layer, half):
    n_seq, lq, d = h.shape
    tm = min(TOKEN_TILE, lq)
    wg, wu, wd = ffn_w
    d_ff = wg.shape[-1]
    tok = lambda w: pl.BlockSpec((1, tm, w), lambda s, i: (s, i, 0))
    return pl.pallas_call(
        _ffn_out_kernel,
        grid=(n_seq, lq // tm),
        in_specs=[
            tok(d),
            pl.BlockSpec((1, N_MOD, d), lambda s, i: (mod_row(s), 0, 0)),
            tok(mix[0].shape[2]), tok(mix[1].shape[2]), tok(mix[2].shape[2]),
            _resident(w_out.shape[1:], (layer,)),
            _resident((1, d)),
            _resident((d, d_ff), (layer, half)), _resident((d, d_ff), (layer, half)),
            _resident((d_ff, d), (layer, half)),
        ],
        out_specs=tok(d),
        out_shape=jax.ShapeDtypeStruct((n_seq, lq, d), F32),
        compiler_params=_params(2),
        name="ffn_out",
    )(h, mods, *mix, w_out, g.reshape(1, d), wg, wu, wd)


def _pool_kernel(u_ref, w_ref, sc_ref, o_ref, p_ref, a2_ref, a4_ref, a8_ref, *, seq):
    u = u_ref[0]
    width = u.shape[1]
    pad = PAD_ROWS
    span = seq + pad
    zero_pad = jnp.zeros((pad, width), F32)
    zero_edge = jnp.zeros((SUBLANES, width), F32)
    p_ref[0:pad] = zero_pad
    p_ref[pad + seq:2 * pad + seq] = zero_pad
    p_ref[pad:pad + seq] = u
    for ref in (a2_ref, a4_ref, a8_ref):
        ref[0:SUBLANES] = zero_edge
        ref[SUBLANES + span:2 * SUBLANES + span] = zero_edge
    a2_ref[SUBLANES:SUBLANES + span] = p_ref[7:7 + span] + p_ref[8:8 + span]
    a4_ref[SUBLANES:SUBLANES + span] = a2_ref[7:7 + span] + a2_ref[9:9 + span]
    a8_ref[SUBLANES:SUBLANES + span] = a4_ref[6:6 + span] + a4_ref[10:10 + span]
    s16 = a8_ref[pad - 4:pad - 4 + seq] + a8_ref[pad + 4:pad + 4 + seq]
    s2 = a2_ref[pad:pad + seq]
    s4 = a4_ref[pad:pad + seq]
    s8 = a8_ref[pad:pad + seq]
    lane = lax.broadcasted_iota(jnp.int32, (seq, width), 1)
    t = lax.broadcasted_iota(jnp.int32, (seq, width), 0)
    g0, g1, g2 = lane < POOL_GROUP, lane < 2 * POOL_GROUP, lane < 3 * POOL_GROUP
    half = jnp.where(g0, POOL_WINDOWS[0] // 2,
                     jnp.where(g1, POOL_WINDOWS[1] // 2,
                               jnp.where(g2, POOL_WINDOWS[2] // 2, POOL_WINDOWS[3] // 2)))
    count = (jnp.minimum(t + half, seq) - jnp.maximum(t - half, 0)).astype(F32)
    wsum = jnp.where(g0, s2, jnp.where(g1, s4, jnp.where(g2, s8, s16)))
    pooled = wsum / count - u
    o_ref[0] = (_dot(pooled.astype(BF16), w_ref[...]) * sc_ref[...]).astype(BF16)


def _pool_call(u, w_bd, scale):
    n_seq, lq, width = u.shape
    buf = pltpu.VMEM((lq + 2 * PAD_ROWS, width), F32)
    return pl.pallas_call(
        functools.partial(_pool_kernel, seq=lq),
        grid=(n_seq,),
        in_specs=[
            pl.BlockSpec((1, lq, width), lambda s: (s, 0, 0)),
            _resident((width, width)),
            _resident((1, width)),
        ],
        out_specs=pl.BlockSpec((1, lq, width), lambda s: (s, 0, 0)),
        out_shape=jax.ShapeDtypeStruct((n_seq, lq, width), BF16),
        scratch_shapes=[buf, buf, buf, buf],
        compiler_params=_params(1),
        name="pool",
    )(u, w_bd, scale.reshape(1, width))


def _dft_tables(b):
    n = 2 * b
    f = np.arange(b, dtype=np.int64)[:, None]
    s = np.arange(b, dtype=np.int64)[None, :]

    def packed(expo):
        ang = 2.0 * np.pi * ((f * expo) % n) / n
        w = np.concatenate([np.cos(ang), -np.sin(ang)], axis=0)
        w[b] = np.where((expo[0] % 2) == 0, 1.0, -1.0)
        return w

    fwd = packed(s)
    rev = packed(b - 1 - s)
    t = np.arange(b, dtype=np.int64)[:, None]
    fr = np.arange(b, dtype=np.int64)[None, :]
    ang = 2.0 * np.pi * ((fr * t) % n) / n
    inv = np.concatenate([2.0 * np.cos(ang), -2.0 * np.sin(ang)], axis=1) / n
    inv[:, 0] = 1.0 / n
    inv[:, b] = np.where((t[:, 0] % 2) == 0, 1.0, -1.0) / n
    return fwd.astype(np.float32), rev.astype(np.float32), inv.astype(np.float32)


def _hyena_tables(seq):
    t = np.linspace(0.0, 1.0, seq)[:, None]
    bands = (HYENA_EMB_DIM - 1) // 2
    w_ang = 2.0 * np.pi * np.arange(seq) / seq
    freqs = np.linspace(1e-4, bands - 1, bands)
    ang = w_ang[:, None] * freqs[None, :]
    z = np.concatenate([t, np.cos(ang), -np.sin(ang)], axis=-1)
    max_decay = math.log(HYENA_DECAY_TARGET) / HYENA_FAST_DECAY_PCT
    min_decay = math.log(HYENA_DECAY_TARGET) / HYENA_SLOW_DECAY_PCT
    deltas = np.linspace(min_decay, max_decay, HYENA_WIDTH)
    decay = np.exp(-t * np.abs(deltas))
    zz = np.zeros((seq, 2 * LANES))
    zz[:, :HYENA_EMB_DIM] = z
    zz[:-1, LANES:LANES + HYENA_EMB_DIM] = z[1:]
    decay_shift = np.zeros_like(decay)
    decay_shift[:-1] = decay[1:]
    return tuple(a.astype(np.float32) for a in (zz, decay, decay_shift))


def _dot_table(w_hi, w_lo, x):
    x_hi, x_lo = _split_bf16(x)
    return _dot(w_hi, x_hi) + _dot(w_hi, x_lo) + _dot(w_lo, x_hi)


def _filter_kernel(zz_ref, dec_ref, decs_ref, w1_ref, b1_ref, w2_ref, b2_ref, w3_ref, sf_ref,
                   fh_ref, fl_ref, rh_ref, rl_ref, o_ref, filt_ref, *, blk, nblk):
    order = pl.program_id(0)

    @pl.when(order == 0)
    def _():
        sf = sf_ref[...]
        h1 = jnp.sin(sf[0:1] * (_dot3(zz_ref[...], w1_ref[...]) + b1_ref[...]))
        h2 = jnp.sin(sf[1:2] * (_dot3(h1, w2_ref[...]) + b2_ref[...]))
        taps = _dot3(h2, w3_ref[...])
        for o in range(HYENA_ORDER):
            fwd = taps[:, o * HYENA_WIDTH:(o + 1) * HYENA_WIDTH] * dec_ref[...]
            bwd = (taps[:, (HYENA_ORDER + o) * HYENA_WIDTH:(HYENA_ORDER + o + 1) * HYENA_WIDTH]
                   * decs_ref[...])
            norm = (jnp.sum(jnp.abs(fwd), axis=0, keepdims=True)
                    + jnp.sum(jnp.abs(bwd), axis=0, keepdims=True))
            filt_ref[o] = fwd / norm
            filt_ref[HYENA_ORDER + o] = bwd / norm

    fwd = filt_ref[order]
    bwd = filt_ref[HYENA_ORDER + order]
    fh, fl, rh, rl = fh_ref[...], fl_ref[...], rh_ref[...], rl_ref[...]
    spec = {}
    for d in range(nblk):
        spec[d] = _dot_table(fh, fl, fwd[d * blk:(d + 1) * blk])
        spec[-(d + 1)] = _dot_table(rh, rl, bwd[d * blk:(d + 1) * blk])
    row = lax.broadcasted_iota(jnp.int32, (2 * blk, HYENA_WIDTH), 0)
    sign = (1 - 2 * (row & 1)).astype(F32)
    for d in range(-(nblk - 1), nblk):
        o_ref[0, d + nblk - 1] = spec[d] + sign * spec[d - 1]


def _filter_weights(w1, b1, w2, b2, w3, sf):
    fo = HYENA_FILTER_ORDER

    def block_diag(a, b):
        top = jnp.concatenate([a, jnp.zeros((a.shape[0], b.shape[1]), F32)], axis=1)
        bot = jnp.concatenate([jnp.zeros((b.shape[0], a.shape[1]), F32), b], axis=1)
        return jnp.concatenate([top, bot], axis=0)

    w1p = jnp.zeros((LANES, fo), F32).at[:HYENA_EMB_DIM].set(w1)
    w3r = w3.reshape(fo, HYENA_ORDER, 2, HYENA_WIDTH)
    w3f = w3r[:, :, 0].reshape(fo, HYENA_ORDER * HYENA_WIDTH)
    w3b = w3r[:, :, 1].reshape(fo, HYENA_ORDER * HYENA_WIDTH)
    return (block_diag(w1p, w1p), jnp.tile(b1, 2).reshape(1, 2 * fo),
            block_diag(w2, w2), jnp.tile(b2, 2).reshape(1, 2 * fo),
            block_diag(w3f, w3b), jnp.tile(sf, (1, 2)))


def _filter_call(tables, dft, weights, *, seq, blk):
    nblk = seq // blk
    zz, dec, decs = tables
    fh, fl, rh, rl = dft
    fo2 = 2 * HYENA_FILTER_ORDER
    n_taps = 2 * HYENA_ORDER * HYENA_WIDTH
    return pl.pallas_call(
        functools.partial(_filter_kernel, blk=blk, nblk=nblk),
        grid=(HYENA_ORDER,),
        in_specs=[
            _resident((seq, 2 * LANES)),
            _resident((seq, HYENA_WIDTH)), _resident((seq, HYENA_WIDTH)),
            _resident((2 * LANES, fo2)), _resident((1, fo2)),
            _resident((fo2, fo2)), _resident((1, fo2)),
            _resident((fo2, n_taps)),
            _resident((2, fo2)),
            _resident((2 * blk, blk)), _resident((2 * blk, blk)),
            _resident((2 * blk, blk)), _resident((2 * blk, blk)),
        ],
        out_specs=pl.BlockSpec((1, 2 * nblk - 1, 2 * blk, HYENA_WIDTH), lambda o: (o, 0, 0, 0)),
        out_shape=jax.ShapeDtypeStruct((HYENA_ORDER, 2 * nblk - 1, 2 * blk, HYENA_WIDTH), F32),
        scratch_shapes=[pltpu.VMEM((2 * HYENA_ORDER, seq, HYENA_WIDTH), F32)],
        compiler_params=_params(1),
        name="hyena_filter",
    )(zz, dec, decs, *weights, fh, fl, rh, rl)


def _short_conv3(p, cw, cb):
    seq, width = p.shape
    zc = pltpu.roll(p, 1, 0) * cw[0:1] + p * cw[1:2] + pltpu.roll(p, seq - 1, 0) * cw[2:3] + cb
    row = lax.broadcasted_iota(jnp.int32, (SUBLANES, width), 0)
    head = zc[:SUBLANES] - jnp.where(row == 0, p[seq - 1:seq] * cw[0:1], 0.0)
    tail = zc[seq - SUBLANES:] - jnp.where(row == SUBLANES - 1, p[0:1] * cw[2:3], 0.0)
    return jnp.concatenate([head, zc[SUBLANES:seq - SUBLANES], tail], axis=0)


def _hyena_kernel(p_ref, cw_ref, cb_ref, hb_ref, wf_ref, wi_ref, ch_ref, o_ref, *, seq, blk):
    nblk = seq // blk
    zc = _short_conv3(p_ref[0], cw_ref[...], cb_ref[...])
    y = zc[:, :HYENA_WIDTH]
    wf = wf_ref[...]
    wi = wi_ref[...]
    hb = hb_ref[...]
    first = lax.broadcasted_iota(jnp.int32, (blk, HYENA_WIDTH), 0) == 0
    for o in range(HYENA_ORDER):
        gate = zc[:, (o + 1) * HYENA_WIDTH:(o + 2) * HYENA_WIDTH]
        spec = [_dot(wf, y[j * blk:(j + 1) * blk].astype(BF16)) for j in range(nblk)]
        outs = []
        for i in range(nblk):
            acc_r = jnp.zeros((blk, HYENA_WIDTH), F32)
            acc_i = jnp.zeros((blk, HYENA_WIDTH), F32)
            for j in range(nblk):
                cm = ch_ref[o, i - j + nblk - 1]
                ur, ui = spec[j][:blk], spec[j][blk:]
                cr, ci = cm[:blk], cm[blk:]
                ii = ui * ci
                acc_r = acc_r + ur * cr - jnp.where(first, 0.0, ii)
                acc_i = acc_i + jnp.where(first, ii, ur * ci + ui * cr)
            prod = jnp.concatenate([acc_r, acc_i], axis=0).astype(BF16)
            outs.append(_dot(wi, prod))
        conv = outs[0] if nblk == 1 else jnp.concatenate(outs, axis=0)
        y = gate * (conv + hb[o:o + 1] * y)
    o_ref[0] = y.astype(BF16)


def _hyena_call(p, conv_w, conv_b, hy_bias, wf, wi, chat, *, blk):
    n_seq, lq, width = p.shape
    nblk = lq // blk
    return pl.pallas_call(
        functools.partial(_hyena_kernel, seq=lq, blk=blk),
        grid=(n_seq,),
        in_specs=[
            pl.BlockSpec((1, lq, width), lambda s: (s, 0, 0)),
            _resident((3, width)),
            _resident((1, width)),
            _resident((HYENA_ORDER, HYENA_WIDTH)),
            _resident((2 * blk, blk)),
            _resident((blk, 2 * blk)),
            _resident((HYENA_ORDER, 2 * nblk - 1, 2 * blk, HYENA_WIDTH)),
        ],
        out_specs=pl.BlockSpec((1, lq, HYENA_WIDTH), lambda s: (s, 0, 0)),
        out_shape=jax.ShapeDtypeStruct((n_seq, lq, HYENA_WIDTH), BF16),
        compiler_params=_params(1),
        name="hyena",
    )(p, conv_w, conv_b.reshape(1, width), hy_bias, wf, wi, chat)


def _qk(q, k):
    return lax.dot_general(q, k, (((1,), (1,)), ((), ())), preferred_element_type=F32)


def _attn_kernel(*refs, with_latent):
    if with_latent:
        q_ref, k_ref, v_ref, kc_ref, vc_ref, o_ref = refs
    else:
        q_ref, kc_ref, vc_ref, o_ref = refs
    q = q_ref[0]
    tq = q.shape[0]
    sub = min(Q_SUB, tq)
    pieces = [[None] * N_Q_HEADS for _ in range(tq // sub)]
    for kh in range(N_KV_HEADS):
        cols = slice(kh * HEAD_DIM, (kh + 1) * HEAD_DIM)
        k_c, v_c = kc_ref[0][:, cols], vc_ref[0][:, cols]
        if with_latent:
            k_l, v_l = k_ref[0][:, cols], v_ref[0][:, cols]
        for r in range(tq // sub):
            rows = slice(r * sub, (r + 1) * sub)
            qs = jnp.concatenate(
                [q[rows, (kh * GQA_GROUP + g) * HEAD_DIM:(kh * GQA_GROUP + g + 1) * HEAD_DIM]
                 for g in range(GQA_GROUP)], axis=0)
            s_c = _qk(qs, k_c)
            m = jnp.max(s_c, axis=-1, keepdims=True)
            if with_latent:
                s_l = _qk(qs, k_l)
                m = jnp.maximum(m, jnp.max(s_l, axis=-1, keepdims=True))
            p_c = jnp.exp2(s_c - m)
            den = jnp.sum(p_c, axis=-1, keepdims=True)
            o = _dot(p_c.astype(BF16), v_c)
            if with_latent:
                p_l = jnp.exp2(s_l - m)
                den = den + jnp.sum(p_l, axis=-1, keepdims=True)
                o = o + _dot(p_l.astype(BF16), v_l)
            o = o / den
            for g in range(GQA_GROUP):
                pieces[r][kh * GQA_GROUP + g] = o[g * sub:(g + 1) * sub]
    o_ref[0] = jnp.concatenate(
        [jnp.concatenate(row, axis=1) for row in pieces], axis=0).astype(BF16)


def _attn_call(q, kc, vc, k=None, v=None):
    n_seq, lq, _ = q.shape
    tq = min(Q_TILE, lq)
    ctx_len = kc.shape[1]
    with_latent = k is not None
    q_spec = pl.BlockSpec((1, tq, ATTN_WIDTH), lambda s, i: (s, i, 0))
    kv_spec = lambda n: pl.BlockSpec((1, n, KV_WIDTH), lambda s, i: (s, 0, 0))
    if with_latent:
        in_specs = [q_spec, kv_spec(lq), kv_spec(lq), kv_spec(ctx_len), kv_spec(ctx_len)]
        args = (q, k, v, kc, vc)
    else:
        in_specs = [q_spec, kv_spec(ctx_len), kv_spec(ctx_len)]
        args = (q, kc, vc)
    return pl.pallas_call(
        functools.partial(_attn_kernel, with_latent=with_latent),
        grid=(n_seq, lq // tq),
        in_specs=in_specs,
        out_specs=q_spec,
        out_shape=jax.ShapeDtypeStruct((n_seq, lq, ATTN_WIDTH), BF16),
        compiler_params=_params(2),
        name="attn" if with_latent else "attn_ctx",
    )(*args)


def _rope_tables(seq):
    rows = seq // GRID_W
    row = np.repeat(np.arange(rows), GRID_W).astype(np.float64)
    col = np.tile(np.arange(GRID_W), rows).astype(np.float64)
    inv_freq = 1.0 / (ROPE_THETA ** (np.arange(0, AXIS_DIM, 2, dtype=np.float64) / AXIS_DIM))
    ang_r = row[:, None] * inv_freq
    ang_c = col[:, None] * inv_freq
    ang = np.concatenate([ang_r, ang_r, ang_c, ang_c], axis=-1)
    cos, sin = np.cos(ang), np.sin(ang)
    upper = (np.arange(HEAD_DIM) % AXIS_DIM) >= AXIS_DIM // 2
    sin_a = np.where(upper, sin, 0.0)
    sin_b = np.where(upper, 0.0, -sin)
    rep = LANES // HEAD_DIM
    return tuple(jnp.asarray(np.tile(a, (1, rep)).astype(np.float32)) for a in (cos, sin_a, sin_b))


def _identity_tables(seq):
    one = jnp.ones((seq, LANES), F32)
    zero = jnp.zeros((seq, LANES), F32)
    return one, zero, zero


def _dft_operands(blk):
    fwd, rev, inv = (jnp.asarray(a) for a in _dft_tables(blk))
    fh, fl = _split_bf16(fwd)
    rh, rl = _split_bf16(rev)
    return (fh, fl, rh, rl), fh, inv.astype(BF16)


def kernel(x, c, ctx, c_ctx, norm_g, w_mod, b_mod, ffn_w_gate, ffn_w_up, ffn_w_down, w_in, w_out, pool_w, pool_scale, hyena_conv_w, hyena_conv_b, hyena_f_w1, hyena_f_b1, hyena_f_w2, hyena_f_b2, hyena_f_w3, hyena_sin_freq, hyena_bias, q_norm_g, k_norm_g):
    batch, seq, d = x.shape
    ctx_len = ctx.shape[1]
    depth = w_mod.shape[0]

    rows = -(-(batch + 1) // SUBLANES) * SUBLANES
    cvec = jnp.zeros((rows, d), F32).at[:batch].set(c).at[batch].set(c_ctx)
    mods = _mods_call(cvec, w_mod, b_mod).reshape(depth * rows, N_MOD, d)

    blk_x = min(DFT_BLOCK, seq)
    blk_c = min(DFT_BLOCK, ctx_len)
    dft_x = _dft_operands(blk_x)
    dft_c = _dft_operands(blk_c)
    hy_tab_x = tuple(jnp.asarray(a) for a in _hyena_tables(seq))
    hy_tab_c = tuple(jnp.asarray(a) for a in _hyena_tables(ctx_len))
    rope_x = _rope_tables(seq)
    rope_c = _identity_tables(batch * ctx_len)
    seg = np.arange(LANES) // HEAD_DIM
    seg_mean = jnp.asarray((seg[:, None] == seg[None, :]).astype(np.float32) / HEAD_DIM).astype(BF16)
    n_win = len(POOL_WINDOWS)

    ffn_w = tuple(w.astype(BF16) for w in (ffn_w_gate, ffn_w_up, ffn_w_down))
    w_in_b = w_in.astype(BF16)
    w_out_b = w_out.astype(BF16)

    h, hc = x, ctx.reshape(1, batch * ctx_len, d)
    for l in range(depth):
        last = l == depth - 1
        row_x = lambda s, l=l: l * rows + s
        row_c = lambda s, l=l: l * rows + batch
        w_pool = jnp.zeros((n_win, POOL_GROUP, n_win, POOL_GROUP), F32)
        for gi in range(n_win):
            w_pool = w_pool.at[gi, :, gi, :].set(pool_w[l, gi])
        w_pool = w_pool.reshape(POOL_WIDTH, POOL_WIDTH).astype(BF16)
        qg = jnp.tile(q_norm_g[l], LANES // HEAD_DIM).reshape(1, LANES)
        kg = jnp.tile(k_norm_g[l], LANES // HEAD_DIM).reshape(1, LANES)
        filt_w = _filter_weights(hyena_f_w1[l], hyena_f_b1[l], hyena_f_w2[l], hyena_f_b2[l],
                                 hyena_f_w3[l], hyena_sin_freq[l])

        h, pool_x, hy_x, q_x, k_x, v_x = _ffn_in_call(
            h, mods, norm_g[l, 0], ffn_w, norm_g[l, 1], w_in_b, rope_x, qg, kg, seg_mean,
            mod_row=row_x, layer=l, half=0)
        hc, *proj_c = _ffn_in_call(
            hc, mods, norm_g[l, 0], ffn_w, norm_g[l, 1], w_in_b, rope_c, qg, kg, seg_mean,
            mod_row=row_c, layer=l, half=0)
        pool_c, hy_c, q_c, k_c, v_c = (a.reshape(batch, ctx_len, a.shape[-1]) for a in proj_c)

        chat_x = _filter_call(hy_tab_x, dft_x[0], filt_w, seq=seq, blk=blk_x)
        mix_x = (
            _pool_call(pool_x, w_pool, pool_scale[l]),
            _hyena_call(hy_x, hyena_conv_w[l], hyena_conv_b[l], hyena_bias[l], dft_x[1], dft_x[2],
                        chat_x, blk=blk_x),
            _attn_call(q_x, k_c, v_c, k_x, v_x),
        )
        if not last:
            chat_c = _filter_call(hy_tab_c, dft_c[0], filt_w, seq=ctx_len, blk=blk_c)
            mix_c = (
                _pool_call(pool_c, w_pool, pool_scale[l]),
                _hyena_call(hy_c, hyena_conv_w[l], hyena_conv_b[l], hyena_bias[l], dft_c[1], dft_c[2],
                            chat_c, blk=blk_c),
                _attn_call(q_c, k_c, v_c),
            )
            mix_c = tuple(a.reshape(1, batch * ctx_len, a.shape[-1]) for a in mix_c)
            hc = _ffn_out_call(hc, mods, mix_c, w_out_b, norm_g[l, 2], ffn_w,
                               mod_row=row_c, layer=l, half=1)
        h = _ffn_out_call(h, mods, mix_x, w_out_b, norm_g[l, 2], ffn_w,
                          mod_row=row_x, layer=l, half=1)
    return h
```

```python
import functools
import math

import jax
import jax.numpy as jnp
import numpy as np
from jax import lax
from jax.experimental import pallas as pl
from jax.experimental.pallas import tpu as pltpu

F32 = jnp.float32
BF16 = jnp.bfloat16

GRID_W = 64
EPS = 1e-6
N_MOD = 9
POOL_WIDTH = 256
POOL_WINDOWS = (2, 4, 8, 16)
POOL_GROUP = POOL_WIDTH // len(POOL_WINDOWS)
HYENA_WIDTH = 256
HYENA_ORDER = 2
HYENA_EMB_DIM = 33
HYENA_FILTER_ORDER = 64
HYENA_DECAY_TARGET = 1e-2
HYENA_FAST_DECAY_PCT = 0.3
HYENA_SLOW_DECAY_PCT = 1.5
HEAD_DIM = 64
N_Q_HEADS = 8
N_KV_HEADS = 2
GQA_GROUP = N_Q_HEADS // N_KV_HEADS
ATTN_WIDTH = N_Q_HEADS * HEAD_DIM
KV_WIDTH = N_KV_HEADS * HEAD_DIM
AXIS_DIM = HEAD_DIM // 2
ROPE_THETA = 10000.0
ATTN_SCALE = HEAD_DIM ** -0.5
Q_PRESCALE = ATTN_SCALE * math.log2(math.e)
HYENA_OFF = POOL_WIDTH
Q_OFF = HYENA_OFF + (HYENA_ORDER + 1) * HYENA_WIDTH
K_OFF = Q_OFF + ATTN_WIDTH
V_OFF = K_OFF + KV_WIDTH
IN_WIDTH = V_OFF + KV_WIDTH

LANES = 128
SUBLANES = 8
VMEM_LIMIT_BYTES = 56 * 1024 * 1024

TOKEN_TILE = 512
Q_TILE = 512
Q_SUB = 128
DFT_BLOCK = 512
MOD_COL_TILE = 1152
PAD_ROWS = 16


def _params(n_axes):
    return pltpu.CompilerParams(
        dimension_semantics=("arbitrary",) * n_axes, vmem_limit_bytes=VMEM_LIMIT_BYTES)


def _resident(shape, lead=()):
    index = tuple(lead) + (0,) * len(shape)
    return pl.BlockSpec((None,) * len(lead) + tuple(shape), lambda *_: index,
                        pipeline_mode=pl.Buffered(1))


def _split_bf16(a):
    hi = a.astype(BF16)
    lo = (a - hi.astype(F32)).astype(BF16)
    return hi, lo


def _dot(a, b):
    return jnp.dot(a, b, preferred_element_type=F32)


def _dot3(a, b):
    a_hi, a_lo = _split_bf16(a)
    b_hi, b_lo = _split_bf16(b)
    return _dot(a_hi, b_hi) + _dot(a_lo, b_hi) + _dot(a_hi, b_lo)


def _norm_mod(x, g, shift, scale):
    ms = jnp.mean(x * x, axis=-1, keepdims=True)
    y = x * lax.rsqrt(ms + EPS) * g
    return y * (1.0 + scale) + shift


def _mods_kernel(c_ref, w_ref, b_ref, o_ref):
    rows = c_ref.shape[0]
    cc = c_ref[...]
    act = cc * jax.nn.sigmoid(cc)
    a_hi, a_lo = _split_bf16(act)
    w_hi, w_lo = _split_bf16(w_ref[0])
    r = _dot(jnp.concatenate([a_hi, a_lo], axis=0), w_hi)
    o_ref[0] = r[:rows] + r[rows:] + _dot(a_hi, w_lo) + b_ref[0]


def _mods_call(cvec, w_mod, b_mod):
    depth, d, n = w_mod.shape
    rows = cvec.shape[0]
    tn = MOD_COL_TILE
    return pl.pallas_call(
        _mods_kernel,
        grid=(depth, n // tn),
        in_specs=[
            pl.BlockSpec((rows, d), lambda l, j: (0, 0)),
            pl.BlockSpec((1, d, tn), lambda l, j: (l, 0, j)),
            pl.BlockSpec((1, 1, tn), lambda l, j: (l, 0, j)),
        ],
        out_specs=pl.BlockSpec((1, rows, tn), lambda l, j: (l, 0, j)),
        out_shape=jax.ShapeDtypeStruct((depth, rows, n), F32),
        compiler_params=_params(2),
        name="mods",
    )(cvec, w_mod, b_mod.reshape(depth, 1, n))


FF_CHUNK = 1024


def _ffn_body(x, m, g, wg_ref, wu_ref, wd_ref, mod_off):
    y = _norm_mod(x, g, m[mod_off:mod_off + 1], m[mod_off + 1:mod_off + 2]).astype(BF16)
    d_ff = wg_ref.shape[1]
    acc = jnp.zeros(x.shape, F32)
    for lo in range(0, d_ff, FF_CHUNK):
        hi = min(lo + FF_CHUNK, d_ff)
        a = _dot(y, wg_ref[:, lo:hi])
        u = _dot(y, wu_ref[:, lo:hi])
        mid = (a * jax.nn.sigmoid(a) * u).astype(BF16)
        acc = acc + _dot(mid, wd_ref[lo:hi, :])
    return x + 0.5 * m[mod_off + 2:mod_off + 3] * acc


def _head_norm_rope(x, g, seg_mean, cos, sin_a, sin_b):
    sq_hi, sq_lo = _split_bf16(x * x)
    ms = _dot(sq_hi, seg_mean) + _dot(sq_lo, seg_mean)
    y = x * lax.rsqrt(ms + EPS) * g
    return (y * cos + pltpu.roll(y, AXIS_DIM // 2, 1) * sin_a
            + pltpu.roll(y, LANES - AXIS_DIM // 2, 1) * sin_b)


def _ffn_in_kernel(x_ref, mod_ref, g0_ref, wg_ref, wu_ref, wd_ref, g1_ref, w_ref,
                   cos_ref, sa_ref, sb_ref, qg_ref, kg_ref, sm_ref,
                   h_ref, pool_ref, hy_ref, q_ref, k_ref, v_ref):
    m = mod_ref[0]
    h = _ffn_body(x_ref[0], m, g0_ref[...], wg_ref, wu_ref, wd_ref, 0)
    h_ref[0] = h
    y = _norm_mod(h, g1_ref[...], m[3:4], m[4:5]).astype(BF16)
    p = _dot(y, w_ref[...])
    pool_ref[0] = p[:, :HYENA_OFF]
    hy_ref[0] = p[:, HYENA_OFF:Q_OFF]
    cos, sa, sb, sm = cos_ref[...], sa_ref[...], sb_ref[...], sm_ref[...]
    qs = []
    for j in range(ATTN_WIDTH // LANES):
        qj = p[:, Q_OFF + j * LANES:Q_OFF + (j + 1) * LANES]
        qs.append(_head_norm_rope(qj, qg_ref[...], sm, cos, sa, sb) * Q_PRESCALE)
    q_ref[0] = jnp.concatenate(qs, axis=1).astype(BF16)
    k_ref[0] = _head_norm_rope(p[:, K_OFF:V_OFF], kg_ref[...], sm, cos, sa, sb).astype(BF16)
    v_ref[0] = p[:, V_OFF:].astype(BF16)


def _ffn_in_call(h, mods, g0, ffn_w, g1, w_in, tables, qg, kg, seg_mean, *, mod_row, layer, half):
    n_seq, lq, d = h.shape
    tm = min(TOKEN_TILE, lq)
    wg, wu, wd = ffn_w
    d_ff = wg.shape[-1]
    cos, sa, sb = tables
    tab_spec = pl.BlockSpec((tm, LANES), lambda s, i: (i, 0))
    tok = lambda w: pl.BlockSpec((1, tm, w), lambda s, i: (s, i, 0))
    shp = lambda w, dt: jax.ShapeDtypeStruct((n_seq, lq, w), dt)
    return pl.pallas_call(
        _ffn_in_kernel,
        grid=(n_seq, lq // tm),
        in_specs=[
            tok(d),
            pl.BlockSpec((1, N_MOD, d), lambda s, i: (mod_row(s), 0, 0)),
            _resident((1, d)),
            _resident((d, d_ff), (layer, half)), _resident((d, d_ff), (layer, half)),
            _resident((d_ff, d), (layer, half)),
            _resident((1, d)),
            _resident((d, IN_WIDTH), (layer,)),
            tab_spec, tab_spec, tab_spec,
            _resident((1, LANES)), _resident((1, LANES)), _resident((LANES, LANES)),
        ],
        out_specs=[tok(d), tok(POOL_WIDTH), tok(Q_OFF - HYENA_OFF), tok(ATTN_WIDTH),
                   tok(KV_WIDTH), tok(KV_WIDTH)],
        out_shape=[shp(d, F32), shp(POOL_WIDTH, F32), shp(Q_OFF - HYENA_OFF, F32),
                   shp(ATTN_WIDTH, BF16), shp(KV_WIDTH, BF16), shp(KV_WIDTH, BF16)],
        compiler_params=_params(2),
        name="ffn_in",
    )(h, mods, g0.reshape(1, d), wg, wu, wd, g1.reshape(1, d), w_in, cos, sa, sb, qg, kg, seg_mean)


def _ffn_out_kernel(h_ref, mod_ref, a_ref, b_ref, c_ref, w_ref, g_ref, wg_ref, wu_ref, wd_ref, o_ref):
    m = mod_ref[0]
    wa = a_ref.shape[2]
    wb = b_ref.shape[2]
    mix = (_dot(a_ref[0], w_ref[0:wa, :]) + _dot(b_ref[0], w_ref[wa:wa + wb, :])
           + _dot(c_ref[0], w_ref[wa + wb:, :]))
    h = h_ref[0] + m[5:6] * mix
    o_ref[0] = _ffn_body(h, m, g_ref[...], wg_ref, wu_ref, wd_ref, 6)


def _ffn_out_call(h, mods, mix, w_out, g, ffn_w, *, mod_row, layer, half):
    n_seq, lq, d = h.shape
    tm = min(TOKEN_TILE, lq)
    wg, wu, wd = ffn_w
    d_ff = wg.shape[-1]
    tok = lambda w: pl.BlockSpec((1, tm, w), lambda s, i: (s, i, 0))
    return pl.pallas_call(
        _ffn_out_kernel,
        grid=(n_seq, lq // tm),
        in_specs=[
            tok(d),
            pl.BlockSpec((1, N_MOD, d), lambda s, i: (mod_row(s), 0, 0)),
            tok(mix[0].shape[2]), tok(mix[1].shape[2]), tok(mix[2].shape[2]),
            _resident(w_out.shape[1:], (layer,)),
            _resident((1, d)),
            _resident((d, d_ff), (layer, half)), _resident((d, d_ff), (layer, half)),
            _resident((d_ff, d), (layer, half)),
        ],
        out_specs=tok(d),
        out_shape=jax.ShapeDtypeStruct((n_seq, lq, d), F32),
        compiler_params=_params(2),
        name="ffn_out",
    )(h, mods, *mix, w_out, g.reshape(1, d), wg, wu, wd)


def _pool_kernel(u_ref, w_ref, sc_ref, o_ref, p_ref, a2_ref, a4_ref, a8_ref, *, seq):
    u = u_ref[0]
    width = u.shape[1]
    pad = PAD_ROWS
    span = seq + pad
    zero_pad = jnp.zeros((pad, width), F32)
    zero_edge = jnp.zeros((SUBLANES, width), F32)
    p_ref[0:pad] = zero_pad
    p_ref[pad + seq:2 * pad + seq] = zero_pad
    p_ref[pad:pad + seq] = u
    for ref in (a2_ref, a4_ref, a8_ref):
        ref[0:SUBLANES] = zero_edge
        ref[SUBLANES + span:2 * SUBLANES + span] = zero_edge
    a2_ref[SUBLANES:SUBLANES + span] = p_ref[7:7 + span] + p_ref[8:8 + span]
    a4_ref[SUBLANES:SUBLANES + span] = a2_ref[7:7 + span] + a2_ref[9:9 + span]
    a8_ref[SUBLANES:SUBLANES + span] = a4_ref[6:6 + span] + a4_ref[10:10 + span]
    s16 = a8_ref[pad - 4:pad - 4 + seq] + a8_ref[pad + 4:pad + 4 + seq]
    s2 = a2_ref[pad:pad + seq]
    s4 = a4_ref[pad:pad + seq]
    s8 = a8_ref[pad:pad + seq]
    lane = lax.broadcasted_iota(jnp.int32, (seq, width), 1)
    t = lax.broadcasted_iota(jnp.int32, (seq, width), 0)
    g0, g1, g2 = lane < POOL_GROUP, lane < 2 * POOL_GROUP, lane < 3 * POOL_GROUP
    half = jnp.where(g0, POOL_WINDOWS[0] // 2,
                     jnp.where(g1, POOL_WINDOWS[1] // 2,
                               jnp.where(g2, POOL_WINDOWS[2] // 2, POOL_WINDOWS[3] // 2)))
    count = (jnp.minimum(t + half, seq) - jnp.maximum(t - half, 0)).astype(F32)
    wsum = jnp.where(g0, s2, jnp.where(g1, s4, jnp.where(g2, s8, s16)))
    pooled = wsum / count - u
    o_ref[0] = (_dot(pooled.astype(BF16), w_ref[...]) * sc_ref[...]).astype(BF16)


def _pool_call(u, w_bd, scale):
    n_seq, lq, width = u.shape
    buf = pltpu.VMEM((lq + 2 * PAD_ROWS, width), F32)
    return pl.pallas_call(
        functools.partial(_pool_kernel, seq=lq),
        grid=(n_seq,),
        in_specs=[
            pl.BlockSpec((1, lq, width), lambda s: (s, 0, 0)),
            _resident((width, width)),
            _resident((1, width)),
        ],
        out_specs=pl.BlockSpec((1, lq, width), lambda s: (s, 0, 0)),
        out_shape=jax.ShapeDtypeStruct((n_seq, lq, width), BF16),
        scratch_shapes=[buf, buf, buf, buf],
        compiler_params=_params(1),
        name="pool",
    )(u, w_bd, scale.reshape(1, width))


def _dft_tables(b):
    n = 2 * b
    f = np.arange(b, dtype=np.int64)[:, None]
    s = np.arange(b, dtype=np.int64)[None, :]

    def packed(expo):
        ang = 2.0 * np.pi * ((f * expo) % n) / n
        w = np.concatenate([np.cos(ang), -np.sin(ang)], axis=0)
        w[b] = np.where((expo[0] % 2) == 0, 1.0, -1.0)
        return w

    fwd = packed(s)
    rev = packed(b - 1 - s)
    t = np.arange(b, dtype=np.int64)[:, None]
    fr = np.arange(b, dtype=np.int64)[None, :]
    ang = 2.0 * np.pi * ((fr * t) % n) / n
    inv = np.concatenate([2.0 * np.cos(ang), -2.0 * np.sin(ang)], axis=1) / n
    inv[:, 0] = 1.0 / n
    inv[:, b] = np.where((t[:, 0] % 2) == 0, 1.0, -1.0) / n
    return fwd.astype(np.float32), rev.astype(np.float32), inv.astype(np.float32)


def _hyena_tables(seq):
    t = np.linspace(0.0, 1.0, seq)[:, None]
    bands = (HYENA_EMB_DIM - 1) // 2
    w_ang = 2.0 * np.pi * np.arange(seq) / seq
    freqs = np.linspace(1e-4, bands - 1, bands)
    ang = w_ang[:, None] * freqs[None, :]
    z = np.concatenate([t, np.cos(ang), -np.sin(ang)], axis=-1)
    max_decay = math.log(HYENA_DECAY_TARGET) / HYENA_FAST_DECAY_PCT
    min_decay = math.log(HYENA_DECAY_TARGET) / HYENA_SLOW_DECAY_PCT
    deltas = np.linspace(min_decay, max_decay, HYENA_WIDTH)
    decay = np.exp(-t * np.abs(deltas))
    zz = np.zeros((seq, 2 * LANES))
    zz[:, :HYENA_EMB_DIM] = z
    zz[:-1, LANES:LANES + HYENA_EMB_DIM] = z[1:]
    decay_shift = np.zeros_like(decay)
    decay_shift[:-1] = decay[1:]
    return tuple(a.astype(np.float32) for a in (zz, decay, decay_shift))


def _dot_table(w_hi, w_lo, x):
    x_hi, x_lo = _split_bf16(x)
    return _dot(w_hi, x_hi) + _dot(w_hi, x_lo) + _dot(w_lo, x_hi)


def _filter_kernel(zz_ref, dec_ref, decs_ref, w1_ref, b1_ref, w2_ref, b2_ref, w3_ref, sf_ref,
                   fh_ref, fl_ref, rh_ref, rl_ref, o_ref, filt_ref, *, blk, nblk):
    order = pl.program_id(0)

    @pl.when(order == 0)
    def _():
        sf = sf_ref[...]
        h1 = jnp.sin(sf[0:1] * (_dot3(zz_ref[...], w1_ref[...]) + b1_ref[...]))
        h2 = jnp.sin(sf[1:2] * (_dot3(h1, w2_ref[...]) + b2_ref[...]))
        taps = _dot3(h2, w3_ref[...])
        for o in range(HYENA_ORDER):
            fwd = taps[:, o * HYENA_WIDTH:(o + 1) * HYENA_WIDTH] * dec_ref[...]
            bwd = (taps[:, (HYENA_ORDER + o) * HYENA_WIDTH:(HYENA_ORDER + o + 1) * HYENA_WIDTH]
                   * decs_ref[...])
            norm = (jnp.sum(jnp.abs(fwd), axis=0, keepdims=True)
                    + jnp.sum(jnp.abs(bwd), axis=0, keepdims=True))
            filt_ref[o] = fwd / norm
            filt_ref[HYENA_ORDER + o] = bwd / norm

    fwd = filt_ref[order]
    bwd = filt_ref[HYENA_ORDER + order]
    fh, fl, rh, rl = fh_ref[...], fl_ref[...], rh_ref[...], rl_ref[...]
    spec = {}
    for d in range(nblk):
        spec[d] = _dot_table(fh, fl, fwd[d * blk:(d + 1) * blk])
        spec[-(d + 1)] = _dot_table(rh, rl, bwd[d * blk:(d + 1) * blk])
    row = lax.broadcasted_iota(jnp.int32, (2 * blk, HYENA_WIDTH), 0)
    sign = (1 - 2 * (row & 1)).astype(F32)
    for d in range(-(nblk - 1), nblk):
        o_ref[0, d + nblk - 1] = spec[d] + sign * spec[d - 1]


def _filter_weights(w1, b1, w2, b2, w3, sf):
    fo = HYENA_FILTER_ORDER

    def block_diag(a, b):
        top = jnp.concatenate([a, jnp.zeros((a.shape[0], b.shape[1]), F32)], axis=1)
        bot = jnp.concatenate([jnp.zeros((b.shape[0], a.shape[1]), F32), b], axis=1)
        return jnp.concatenate([top, bot], axis=0)

    w1p = jnp.zeros((LANES, fo), F32).at[:HYENA_EMB_DIM].set(w1)
    w3r = w3.reshape(fo, HYENA_ORDER, 2, HYENA_WIDTH)
    w3f = w3r[:, :, 0].reshape(fo, HYENA_ORDER * HYENA_WIDTH)
    w3b = w3r[:, :, 1].reshape(fo, HYENA_ORDER * HYENA_WIDTH)
    return (block_diag(w1p, w1p), jnp.tile(b1, 2).reshape(1, 2 * fo),
            block_diag(w2, w2), jnp.tile(b2, 2).reshape(1, 2 * fo),
            block_diag(w3f, w3b), jnp.tile(sf, (1, 2)))


def _filter_call(tables, dft, weights, *, seq, blk):
    nblk = seq // blk
    zz, dec, decs = tables
    fh, fl, rh, rl = dft
    fo2 = 2 * HYENA_FILTER_ORDER
    n_taps = 2 * HYENA_ORDER * HYENA_WIDTH
    return pl.pallas_call(
        functools.partial(_filter_kernel, blk=blk, nblk=nblk),
        grid=(HYENA_ORDER,),
        in_specs=[
            _resident((seq, 2 * LANES)),
            _resident((seq, HYENA_WIDTH)), _resident((seq, HYENA_WIDTH)),
            _resident((2 * LANES, fo2)), _resident((1, fo2)),
            _resident((fo2, fo2)), _resident((1, fo2)),
            _resident((fo2, n_taps)),
            _resident((2, fo2)),
            _resident((2 * blk, blk)), _resident((2 * blk, blk)),
            _resident((2 * blk, blk)), _resident((2 * blk, blk)),
        ],
        out_specs=pl.BlockSpec((1, 2 * nblk - 1, 2 * blk, HYENA_WIDTH), lambda o: (o, 0, 0, 0)),
        out_shape=jax.ShapeDtypeStruct((HYENA_ORDER, 2 * nblk - 1, 2 * blk, HYENA_WIDTH), F32),
        scratch_shapes=[pltpu.VMEM((2 * HYENA_ORDER, seq, HYENA_WIDTH), F32)],
        compiler_params=_params(1),
        name="hyena_filter",
    )(zz, dec, decs, *weights, fh, fl, rh, rl)


def _short_conv3(p, cw, cb):
    seq, width = p.shape
    zc = pltpu.roll(p, 1, 0) * cw[0:1] + p * cw[1:2] + pltpu.roll(p, seq - 1, 0) * cw[2:3] + cb
    row = lax.broadcasted_iota(jnp.int32, (SUBLANES, width), 0)
    head = zc[:SUBLANES] - jnp.where(row == 0, p[seq - 1:seq] * cw[0:1], 0.0)
    tail = zc[seq - SUBLANES:] - jnp.where(row == SUBLANES - 1, p[0:1] * cw[2:3], 0.0)
    return jnp.concatenate([head, zc[SUBLANES:seq - SUBLANES], tail], axis=0)


def _hyena_kernel(p_ref, cw_ref, cb_ref, hb_ref, wf_ref, wi_ref, ch_ref, o_ref, *, seq, blk):
    nblk = seq // blk
    zc = _short_conv3(p_ref[0], cw_ref[...], cb_ref[...])
    y = zc[:, :HYENA_WIDTH]
    wf = wf_ref[...]
    wi = wi_ref[...]
    hb = hb_ref[...]
    first = lax.broadcasted_iota(jnp.int32, (blk, HYENA_WIDTH), 0) == 0
    for o in range(HYENA_ORDER):
        gate = zc[:, (o + 1) * HYENA_WIDTH:(o + 2) * HYENA_WIDTH]
        spec = [_dot(wf, y[j * blk:(j + 1) * blk].astype(BF16)) for j in range(nblk)]
        outs = []
        for i in range(nblk):
            acc_r = jnp.zeros((blk, HYENA_WIDTH), F32)
            acc_i = jnp.zeros((blk, HYENA_WIDTH), F32)
            for j in range(nblk):
                cm = ch_ref[o, i - j + nblk - 1]
                ur, ui = spec[j][:blk], spec[j][blk:]
                cr, ci = cm[:blk], cm[blk:]
                ii = ui * ci
                acc_r = acc_r + ur * cr - jnp.where(first, 0.0, ii)
                acc_i = acc_i + jnp.where(first, ii, ur * ci + ui * cr)
            prod = jnp.concatenate([acc_r, acc_i], axis=0).astype(BF16)
            outs.append(_dot(wi, prod))
        conv = outs[0] if nblk == 1 else jnp.concatenate(outs, axis=0)
        y = gate * (conv + hb[o:o + 1] * y)
    o_ref[0] = y.astype(BF16)


def _hyena_call(p, conv_w, conv_b, hy_bias, wf, wi, chat, *, blk):
    n_seq, lq, width = p.shape
    nblk = lq // blk
    return pl.pallas_call(
        functools.partial(_hyena_kernel, seq=lq, blk=blk),
        grid=(n_seq,),
        in_specs=[
            pl.BlockSpec((1, lq, width), lambda s: (s, 0, 0)),
            _resident((3, width)),
            _resident((1, width)),
            _resident((HYENA_ORDER, HYENA_WIDTH)),
            _resident((2 * blk, blk)),
            _resident((blk, 2 * blk)),
            _resident((HYENA_ORDER, 2 * nblk - 1, 2 * blk, HYENA_WIDTH)),
        ],
        out_specs=pl.BlockSpec((1, lq, HYENA_WIDTH), lambda s: (s, 0, 0)),
        out_shape=jax.ShapeDtypeStruct((n_seq, lq, HYENA_WIDTH), BF16),
        compiler_params=_params(1),
        name="hyena",
    )(p, conv_w, conv_b.reshape(1, width), hy_bias, wf, wi, chat)


def _qk(q, k):
    return lax.dot_general(q, k, (((1,), (1,)), ((), ())), preferred_element_type=F32)


def _attn_kernel(*refs, with_latent):
    if with_latent:
        q_ref, k_ref, v_ref, kc_ref, vc_ref, o_ref = refs
    else:
        q_ref, kc_ref, vc_ref, o_ref = refs
    q = q_ref[0]
    tq = q.shape[0]
    sub = min(Q_SUB, tq)
    pieces = [[None] * N_Q_HEADS for _ in range(tq // sub)]
    for kh in range(N_KV_HEADS):
        cols = slice(kh * HEAD_DIM, (kh + 1) * HEAD_DIM)
        ext = lambda v: jnp.concatenate([v, jnp.ones_like(v)], axis=1)
        k_c, v_c = kc_ref[0][:, cols], ext(vc_ref[0][:, cols])
        if with_latent:
            k_l, v_l = k_ref[0][:, cols], ext(v_ref[0][:, cols])
        for r in range(tq // sub):
            rows = slice(r * sub, (r + 1) * sub)
            qs = jnp.concatenate(
                [q[rows, (kh * GQA_GROUP + g) * HEAD_DIM:(kh * GQA_GROUP + g + 1) * HEAD_DIM]
                 for g in range(GQA_GROUP)], axis=0)
            s_c = _qk(qs, k_c)
            m = jnp.max(s_c, axis=-1, keepdims=True)
            if with_latent:
                s_l = _qk(qs, k_l)
                m = jnp.maximum(m, jnp.max(s_l, axis=-1, keepdims=True))
            o = _dot(jnp.exp2(s_c - m).astype(BF16), v_c)
            if with_latent:
                o = o + _dot(jnp.exp2(s_l - m).astype(BF16), v_l)
            o = o[:, :HEAD_DIM] / o[:, HEAD_DIM:]
            for g in range(GQA_GROUP):
                pieces[r][kh * GQA_GROUP + g] = o[g * sub:(g + 1) * sub]
    o_ref[0] = jnp.concatenate(
        [jnp.concatenate(row, axis=1) for row in pieces], axis=0).astype(BF16)


def _attn_call(q, kc, vc, k=None, v=None):
    n_seq, lq, _ = q.shape
    tq = min(Q_TILE, lq)
    ctx_len = kc.shape[1]
    with_latent = k is not None
    q_spec = pl.BlockSpec((1, tq, ATTN_WIDTH), lambda s, i: (s, i, 0))
    kv_spec = lambda n: pl.BlockSpec((1, n, KV_WIDTH), lambda s, i: (s, 0, 0))
    if with_latent:
        in_specs = [q_spec, kv_spec(lq), kv_spec(lq), kv_spec(ctx_len), kv_spec(ctx_len)]
        args = (q, k, v, kc, vc)
    else:
        in_specs = [q_spec, kv_spec(ctx_len), kv_spec(ctx_len)]
        args = (q, kc, vc)
    return pl.pallas_call(
        functools.partial(_attn_kernel, with_latent=with_latent),
        grid=(n_seq, lq // tq),
        in_specs=in_specs,
        out_specs=q_spec,
        out_shape=jax.ShapeDtypeStruct((n_seq, lq, ATTN_WIDTH), BF16),
        compiler_params=_params(2),
        name="attn" if with_latent else "attn_ctx",
    )(*args)


def _rope_tables(seq):
    rows = seq // GRID_W
    row = np.repeat(np.arange(rows), GRID_W).astype(np.float64)
    col = np.tile(np.arange(GRID_W), rows).astype(np.float64)
    inv_freq = 1.0 / (ROPE_THETA ** (np.arange(0, AXIS_DIM, 2, dtype=np.float64) / AXIS_DIM))
    ang_r = row[:, None] * inv_freq
    ang_c = col[:, None] * inv_freq
    ang = np.concatenate([ang_r, ang_r, ang_c, ang_c], axis=-1)
    cos, sin = np.cos(ang), np.sin(ang)
    upper = (np.arange(HEAD_DIM) % AXIS_DIM) >= AXIS_DIM // 2
    sin_a = np.where(upper, sin, 0.0)
    sin_b = np.where(upper, 0.0, -sin)
    rep = LANES // HEAD_DIM
    return tuple(jnp.asarray(np.tile(a, (1, rep)).astype(np.float32)) for a in (cos, sin_a, sin_b))


def _identity_tables(seq):
    one = jnp.ones((seq, LANES), F32)
    zero = jnp.zeros((seq, LANES), F32)
    return one, zero, zero


def _dft_operands(blk):
    fwd, rev, inv = (jnp.asarray(a) for a in _dft_tables(blk))
    fh, fl = _split_bf16(fwd)
    rh, rl = _split_bf16(rev)
    return (fh, fl, rh, rl), fh, inv.astype(BF16)


def kernel(x, c, ctx, c_ctx, norm_g, w_mod, b_mod, ffn_w_gate, ffn_w_up, ffn_w_down, w_in, w_out, pool_w, pool_scale, hyena_conv_w, hyena_conv_b, hyena_f_w1, hyena_f_b1, hyena_f_w2, hyena_f_b2, hyena_f_w3, hyena_sin_freq, hyena_bias, q_norm_g, k_norm_g):
    batch, seq, d = x.shape
    ctx_len = ctx.shape[1]
    depth = w_mod.shape[0]

    rows = -(-(batch + 1) // SUBLANES) * SUBLANES
    cvec = jnp.zeros((rows, d), F32).at[:batch].set(c).at[batch].set(c_ctx)
    mods = _mods_call(cvec, w_mod, b_mod).reshape(depth * rows, N_MOD, d)

    blk_x = min(DFT_BLOCK, seq)
    blk_c = min(DFT_BLOCK, ctx_len)
    dft_x = _dft_operands(blk_x)
    dft_c = _dft_operands(blk_c)
    hy_tab_x = tuple(jnp.asarray(a) for a in _hyena_tables(seq))
    hy_tab_c = tuple(jnp.asarray(a) for a in _hyena_tables(ctx_len))
    rope_x = _rope_tables(seq)
    rope_c = _identity_tables(batch * ctx_len)
    seg = np.arange(LANES) // HEAD_DIM
    seg_mean = jnp.asarray((seg[:, None] == seg[None, :]).astype(np.float32) / HEAD_DIM).astype(BF16)
    n_win = len(POOL_WINDOWS)

    ffn_w = tuple(w.astype(BF16) for w in (ffn_w_gate, ffn_w_up, ffn_w_down))
    w_in_b = w_in.astype(BF16)
    w_out_b = w_out.astype(BF16)

    h, hc = x, ctx.reshape(1, batch * ctx_len, d)
    for l in range(depth):
        last = l == depth - 1
        row_x = lambda s, l=l: l * rows + s
        row_c = lambda s, l=l: l * rows + batch
        w_pool = jnp.zeros((n_win, POOL_GROUP, n_win, POOL_GROUP), F32)
        for gi in range(n_win):
            w_pool = w_pool.at[gi, :, gi, :].set(pool_w[l, gi])
        w_pool = w_pool.reshape(POOL_WIDTH, POOL_WIDTH).astype(BF16)
        qg = jnp.tile(q_norm_g[l], LANES // HEAD_DIM).reshape(1, LANES)
        kg = jnp.tile(k_norm_g[l], LANES // HEAD_DIM).reshape(1, LANES)
        filt_w = _filter_weights(hyena_f_w1[l], hyena_f_b1[l], hyena_f_w2[l], hyena_f_b2[l],
                                 hyena_f_w3[l], hyena_sin_freq[l])

        h, pool_x, hy_x, q_x, k_x, v_x = _ffn_in_call(
            h, mods, norm_g[l, 0], ffn_w, norm_g[l, 1], w_in_b, rope_x, qg, kg, seg_mean,
            mod_row=row_x, layer=l, half=0)
        hc, *proj_c = _ffn_in_call(
            hc, mods, norm_g[l, 0], ffn_w, norm_g[l, 1], w_in_b, rope_c, qg, kg, seg_mean,
            mod_row=row_c, layer=l, half=0)
        pool_c, hy_c, q_c, k_c, v_c = (a.reshape(batch, ctx_len, a.shape[-1]) for a in proj_c)

        chat_x = _filter_call(hy_tab_x, dft_x[0], filt_w, seq=seq, blk=blk_x)
        mix_x = (
            _pool_call(pool_x, w_pool, pool_scale[l]),
            _hyena_call(hy_x, hyena_conv_w[l], hyena_conv_b[l], hyena_bias[l], dft_x[1], dft_x[2],
                        chat_x, blk=blk_x),
            _attn_call(q_x, k_c, v_c, k_x, v_x),
        )
        if not last:
            chat_c = _filter_call(hy_tab_c, dft_c[0], filt_w, seq=ctx_len, blk=blk_c)
            mix_c = (
                _pool_call(pool_c, w_pool, pool_scale[l]),
                _hyena_call(hy_c, hyena_conv_w[l], hyena_conv_b[l], hyena_bias[l], dft_c[1], dft_c[2],
                            chat_c, blk=blk_c),
                _attn_call(q_c, k_c, v_c),
            )
            mix_c = tuple(a.reshape(1, batch * ctx_len, a.shape[-1]) for a in mix_c)
            hc = _ffn_out_call(hc, mods, mix_c, w_out_b, norm_g[l, 2], ffn_w,
                               mod_row=row_c, layer=l, half=1)
        h = _ffn_out_call(h, mods, mix_x, w_out_b, norm_g[l, 2], ffn_w,
                          mod_row=row_x, layer=l, half=1)
    return h
```

```python
import functools
import math

import jax
import jax.numpy as jnp
import numpy as np
from jax import lax
from jax.experimental import pallas as pl
from jax.experimental.pallas import tpu as pltpu

F32 = jnp.float32
BF16 = jnp.bfloat16

GRID_W = 64
EPS = 1e-6
N_MOD = 9
POOL_WIDTH = 256
POOL_WINDOWS = (2, 4, 8, 16)
POOL_GROUP = POOL_WIDTH // len(POOL_WINDOWS)
HYENA_WIDTH = 256
HYENA_ORDER = 2
HYENA_EMB_DIM = 33
HYENA_FILTER_ORDER = 64
HYENA_DECAY_TARGET = 1e-2
HYENA_FAST_DECAY_PCT = 0.3
HYENA_SLOW_DECAY_PCT = 1.5
HEAD_DIM = 64
N_Q_HEADS = 8
N_KV_HEADS = 2
GQA_GROUP = N_Q_HEADS // N_KV_HEADS
ATTN_WIDTH = N_Q_HEADS * HEAD_DIM
KV_WIDTH = N_KV_HEADS * HEAD_DIM
AXIS_DIM = HEAD_DIM // 2
ROPE_THETA = 10000.0
ATTN_SCALE = HEAD_DIM ** -0.5
Q_PRESCALE = ATTN_SCALE * math.log2(math.e)
HYENA_OFF = POOL_WIDTH
Q_OFF = HYENA_OFF + (HYENA_ORDER + 1) * HYENA_WIDTH
K_OFF = Q_OFF + ATTN_WIDTH
V_OFF = K_OFF + KV_WIDTH
IN_WIDTH = V_OFF + KV_WIDTH

LANES = 128
SUBLANES = 8
VMEM_LIMIT_BYTES = 56 * 1024 * 1024

TOKEN_TILE = 512
Q_TILE = 512
Q_SUB = 128
DFT_BLOCK = 512
MOD_COL_TILE = 1152
PAD_ROWS = 16


def _params(n_axes):
    return pltpu.CompilerParams(
        dimension_semantics=("arbitrary",) * n_axes, vmem_limit_bytes=VMEM_LIMIT_BYTES)


def _resident(shape, lead=()):
    index = tuple(lead) + (0,) * len(shape)
    return pl.BlockSpec((None,) * len(lead) + tuple(shape), lambda *_: index,
                        pipeline_mode=pl.Buffered(1))


def _split_bf16(a):
    hi = a.astype(BF16)
    lo = (a - hi.astype(F32)).astype(BF16)
    return hi, lo


def _dot(a, b):
    return jnp.dot(a, b, preferred_element_type=F32)


def _dot3(a, b):
    a_hi, a_lo = _split_bf16(a)
    b_hi, b_lo = _split_bf16(b)
    return _dot(a_hi, b_hi) + _dot(a_lo, b_hi) + _dot(a_hi, b_lo)


def _norm_mod(x, g, shift, scale):
    ms = jnp.mean(x * x, axis=-1, keepdims=True)
    y = x * lax.rsqrt(ms + EPS) * g
    return y * (1.0 + scale) + shift


def _mods_kernel(c_ref, w_ref, b_ref, o_ref):
    rows = c_ref.shape[0]
    cc = c_ref[...]
    act = cc * jax.nn.sigmoid(cc)
    a_hi, a_lo = _split_bf16(act)
    w_hi, w_lo = _split_bf16(w_ref[0])
    r = _dot(jnp.concatenate([a_hi, a_lo], axis=0), w_hi)
    o_ref[0] = r[:rows] + r[rows:] + _dot(a_hi, w_lo) + b_ref[0]


def _mods_call(cvec, w_mod, b_mod):
    depth, d, n = w_mod.shape
    rows = cvec.shape[0]
    tn = MOD_COL_TILE
    return pl.pallas_call(
        _mods_kernel,
        grid=(depth, n // tn),
        in_specs=[
            pl.BlockSpec((rows, d), lambda l, j: (0, 0)),
            pl.BlockSpec((1, d, tn), lambda l, j: (l, 0, j)),
            pl.BlockSpec((1, 1, tn), lambda l, j: (l, 0, j)),
        ],
        out_specs=pl.BlockSpec((1, rows, tn), lambda l, j: (l, 0, j)),
        out_shape=jax.ShapeDtypeStruct((depth, rows, n), F32),
        compiler_params=_params(2),
        name="mods",
    )(cvec, w_mod, b_mod.reshape(depth, 1, n))


FF_CHUNK = 1024


def _ffn_body(x, m, g, wg_ref, wu_ref, wd_ref, mod_off):
    y = _norm_mod(x, g, m[mod_off:mod_off + 1], m[mod_off + 1:mod_off + 2]).astype(BF16)
    d_ff = wg_ref.shape[1]
    acc = jnp.zeros(x.shape, F32)
    for lo in range(0, d_ff, FF_CHUNK):
        hi = min(lo + FF_CHUNK, d_ff)
        a = _dot(y, wg_ref[:, lo:hi])
        u = _dot(y, wu_ref[:, lo:hi])
        mid = (a * jax.nn.sigmoid(a) * u).astype(BF16)
        acc = acc + _dot(mid, wd_ref[lo:hi, :])
    return x + 0.5 * m[mod_off + 2:mod_off + 3] * acc


def _head_norm_rope(x, g, seg_mean, cos, sin_a, sin_b):
    sq_hi, sq_lo = _split_bf16(x * x)
    ms = _dot(sq_hi, seg_mean) + _dot(sq_lo, seg_mean)
    y = x * lax.rsqrt(ms + EPS) * g
    return (y * cos + pltpu.roll(y, AXIS_DIM // 2, 1) * sin_a
            + pltpu.roll(y, LANES - AXIS_DIM // 2, 1) * sin_b)


def _ffn_in_kernel(x_ref, mod_ref, g0_ref, wg_ref, wu_ref, wd_ref, g1_ref, w_ref,
                   cos_ref, sa_ref, sb_ref, qg_ref, kg_ref, sm_ref,
                   h_ref, pool_ref, hy_ref, q_ref, k_ref, v_ref):
    m = mod_ref[0]
    h = _ffn_body(x_ref[0], m, g0_ref[...], wg_ref, wu_ref, wd_ref, 0)
    h_ref[0] = h
    y = _norm_mod(h, g1_ref[...], m[3:4], m[4:5]).astype(BF16)
    p = _dot(y, w_ref[...])
    pool_ref[0] = p[:, :HYENA_OFF]
    hy_ref[0] = p[:, HYENA_OFF:Q_OFF]
    cos, sa, sb, sm = cos_ref[...], sa_ref[...], sb_ref[...], sm_ref[...]
    qs = []
    for j in range(ATTN_WIDTH // LANES):
        qj = p[:, Q_OFF + j * LANES:Q_OFF + (j + 1) * LANES]
        qs.append(_head_norm_rope(qj, qg_ref[...], sm, cos, sa, sb) * Q_PRESCALE)
    q_ref[0] = jnp.concatenate(qs, axis=1).astype(BF16)
    k_ref[0] = _head_norm_rope(p[:, K_OFF:V_OFF], kg_ref[...], sm, cos, sa, sb).astype(BF16)
    v_ref[0] = p[:, V_OFF:].astype(BF16)


def _ffn_in_call(h, mods, g0, ffn_w, g1, w_in, tables, qg, kg, seg_mean, *, mod_row, layer, half):
    n_seq, lq, d = h.shape
    tm = min(TOKEN_TILE, lq)
    wg, wu, wd = ffn_w
    d_ff = wg.shape[-1]
    cos, sa, sb = tables
    tab_spec = pl.BlockSpec((tm, LANES), lambda s, i: (i, 0))
    tok = lambda w: pl.BlockSpec((1, tm, w), lambda s, i: (s, i, 0))
    shp = lambda w, dt: jax.ShapeDtypeStruct((n_seq, lq, w), dt)
    return pl.pallas_call(
        _ffn_in_kernel,
        grid=(n_seq, lq // tm),
        in_specs=[
            tok(d),
            pl.BlockSpec((1, N_MOD, d), lambda s, i: (mod_row(s), 0, 0)),
            _resident((1, d)),
            _resident((d, d_ff), (layer, half)), _resident((d, d_ff), (layer, half)),
            _resident((d_ff, d), (layer, half)),
            _resident((1, d)),
            _resident((d, IN_WIDTH), (layer,)),
            tab_spec, tab_spec, tab_spec,
            _resident((1, LANES)), _resident((1, LANES)), _resident((LANES, LANES)),
        ],
        out_specs=[tok(d), tok(POOL_WIDTH), tok(Q_OFF - HYENA_OFF), tok(ATTN_WIDTH),
                   tok(KV_WIDTH), tok(KV_WIDTH)],
        out_shape=[shp(d, F32), shp(POOL_WIDTH, F32), shp(Q_OFF - HYENA_OFF, F32),
                   shp(ATTN_WIDTH, BF16), shp(KV_WIDTH, BF16), shp(KV_WIDTH, BF16)],
        compiler_params=_params(2),
        name="ffn_in",
    )(h, mods, g0.reshape(1, d), wg, wu, wd, g1.reshape(1, d), w_in, cos, sa, sb, qg, kg, seg_mean)


def _ffn_out_kernel(h_ref, mod_ref, a_ref, b_ref, c_ref, w_ref, g_ref, wg_ref, wu_ref, wd_ref, o_ref):
    m = mod_ref[0]
    wa = a_ref.shape[2]
    wb = b_ref.shape[2]
    mix = (_dot(a_ref[0], w_ref[0:wa, :]) + _dot(b_ref[0], w_ref[wa:wa + wb, :])
           + _dot(c_ref[0], w_ref[wa + wb:, :]))
    h = h_ref[0] + m[5:6] * mix
    o_ref[0] = _ffn_body(h, m, g_ref[...], wg_ref, wu_ref, wd_ref, 6)


def _ffn_out_call(h, mods, mix, w_out, g, ffn_w, *, mod_row, layer, half):
    n_seq, lq, d = h.shape
    tm = min(TOKEN_TILE, lq)
    wg, wu, wd = ffn_w
    d_ff = wg.shape[-1]
    tok = lambda w: pl.BlockSpec((1, tm, w), lambda s, i: (s, i, 0))
    return pl.pallas_call(
        _ffn_out_kernel,
        grid=(n_seq, lq // tm),
        in_specs=[
            tok(d),
            pl.BlockSpec((1, N_MOD, d), lambda s, i: (mod_row(s), 0, 0)),
            tok(mix[0].shape[2]), tok(mix[1].shape[2]), tok(mix[2].shape[2]),
            _resident(w_out.shape[1:], (layer,)),
            _resident((1, d)),
            _resident((d, d_ff), (layer, half)), _resident((d, d_ff), (layer, half)),
            _resident((d_ff, d), (layer, half)),
        ],
        out_specs=tok(d),
        out_shape=jax.ShapeDtypeStruct((n_seq, lq, d), F32),
        compiler_params=_params(2),
        name="ffn_out",
    )(h, mods, *mix, w_out, g.reshape(1, d), wg, wu, wd)


def _pool_kernel(u_ref, w_ref, sc_ref, o_ref, p_ref, a2_ref, a4_ref, a8_ref, *, seq):
    u = u_ref[0]
    width = u.shape[1]
    pad = PAD_ROWS
    span = seq + pad
    zero_pad = jnp.zeros((pad, width), F32)
    zero_edge = jnp.zeros((SUBLANES, width), F32)
    p_ref[0:pad] = zero_pad
    p_ref[pad + seq:2 * pad + seq] = zero_pad
    p_ref[pad:pad + seq] = u
    for ref in (a2_ref, a4_ref, a8_ref):
        ref[0:SUBLANES] = zero_edge
        ref[SUBLANES + span:2 * SUBLANES + span] = zero_edge
    a2_ref[SUBLANES:SUBLANES + span] = p_ref[7:7 + span] + p_ref[8:8 + span]
    a4_ref[SUBLANES:SUBLANES + span] = a2_ref[7:7 + span] + a2_ref[9:9 + span]
    a8_ref[SUBLANES:SUBLANES + span] = a4_ref[6:6 + span] + a4_ref[10:10 + span]
    s16 = a8_ref[pad - 4:pad - 4 + seq] + a8_ref[pad + 4:pad + 4 + seq]
    s2 = a2_ref[pad:pad + seq]
    s4 = a4_ref[pad:pad + seq]
    s8 = a8_ref[pad:pad + seq]
    lane = lax.broadcasted_iota(jnp.int32, (seq, width), 1)
    t = lax.broadcasted_iota(jnp.int32, (seq, width), 0)
    g0, g1, g2 = lane < POOL_GROUP, lane < 2 * POOL_GROUP, lane < 3 * POOL_GROUP
    half = jnp.where(g0, POOL_WINDOWS[0] // 2,
                     jnp.where(g1, POOL_WINDOWS[1] // 2,
                               jnp.where(g2, POOL_WINDOWS[2] // 2, POOL_WINDOWS[3] // 2)))
    count = (jnp.minimum(t + half, seq) - jnp.maximum(t - half, 0)).astype(F32)
    wsum = jnp.where(g0, s2, jnp.where(g1, s4, jnp.where(g2, s8, s16)))
    pooled = wsum / count - u
    o_ref[0] = (_dot(pooled.astype(BF16), w_ref[...]) * sc_ref[...]).astype(BF16)


def _pool_call(u, w_bd, scale):
    n_seq, lq, width = u.shape
    buf = pltpu.VMEM((lq + 2 * PAD_ROWS, width), F32)
    return pl.pallas_call(
        functools.partial(_pool_kernel, seq=lq),
        grid=(n_seq,),
        in_specs=[
            pl.BlockSpec((1, lq, width), lambda s: (s, 0, 0)),
            _resident((width, width)),
            _resident((1, width)),
        ],
        out_specs=pl.BlockSpec((1, lq, width), lambda s: (s, 0, 0)),
        out_shape=jax.ShapeDtypeStruct((n_seq, lq, width), BF16),
        scratch_shapes=[buf, buf, buf, buf],
        compiler_params=_params(1),
        name="pool",
    )(u, w_bd, scale.reshape(1, width))


def _dft_tables(b):
    n = 2 * b
    f = np.arange(b, dtype=np.int64)[:, None]
    s = np.arange(b, dtype=np.int64)[None, :]

    def packed(expo):
        ang = 2.0 * np.pi * ((f * expo) % n) / n
        w = np.concatenate([np.cos(ang), -np.sin(ang)], axis=0)
        w[b] = np.where((expo[0] % 2) == 0, 1.0, -1.0)
        return w

    fwd = packed(s)
    rev = packed(b - 1 - s)
    t = np.arange(b, dtype=np.int64)[:, None]
    fr = np.arange(b, dtype=np.int64)[None, :]
    ang = 2.0 * np.pi * ((fr * t) % n) / n
    inv = np.concatenate([2.0 * np.cos(ang), -2.0 * np.sin(ang)], axis=1) / n
    inv[:, 0] = 1.0 / n
    inv[:, b] = np.where((t[:, 0] % 2) == 0, 1.0, -1.0) / n
    return fwd.astype(np.float32), rev.astype(np.float32), inv.astype(np.float32)


def _hyena_tables(seq):
    t = np.linspace(0.0, 1.0, seq)[:, None]
    bands = (HYENA_EMB_DIM - 1) // 2
    w_ang = 2.0 * np.pi * np.arange(seq) / seq
    freqs = np.linspace(1e-4, bands - 1, bands)
    ang = w_ang[:, None] * freqs[None, :]
    z = np.concatenate([t, np.cos(ang), -np.sin(ang)], axis=-1)
    max_decay = math.log(HYENA_DECAY_TARGET) / HYENA_FAST_DECAY_PCT
    min_decay = math.log(HYENA_DECAY_TARGET) / HYENA_SLOW_DECAY_PCT
    deltas = np.linspace(min_decay, max_decay, HYENA_WIDTH)
    decay = np.exp(-t * np.abs(deltas))
    zz = np.zeros((seq, 2 * LANES))
    zz[:, :HYENA_EMB_DIM] = z
    zz[:-1, LANES:LANES + HYENA_EMB_DIM] = z[1:]
    decay_shift = np.zeros_like(decay)
    decay_shift[:-1] = decay[1:]
    return tuple(a.astype(np.float32) for a in (zz, decay, decay_shift))


def _dot_table(w_hi, w_lo, x):
    x_hi, x_lo = _split_bf16(x)
    return _dot(w_hi, x_hi) + _dot(w_hi, x_lo) + _dot(w_lo, x_hi)


def _filter_kernel(zz_ref, dec_ref, decs_ref, w1_ref, b1_ref, w2_ref, b2_ref, w3_ref, sf_ref,
                   fh_ref, fl_ref, rh_ref, rl_ref, o_ref, filt_ref, *, blk, nblk):
    order = pl.program_id(0)

    @pl.when(order == 0)
    def _():
        sf = sf_ref[...]
        h1 = jnp.sin(sf[0:1] * (_dot3(zz_ref[...], w1_ref[...]) + b1_ref[...]))
        h2 = jnp.sin(sf[1:2] * (_dot3(h1, w2_ref[...]) + b2_ref[...]))
        taps = _dot3(h2, w3_ref[...])
        for o in range(HYENA_ORDER):
            fwd = taps[:, o * HYENA_WIDTH:(o + 1) * HYENA_WIDTH] * dec_ref[...]
            bwd = (taps[:, (HYENA_ORDER + o) * HYENA_WIDTH:(HYENA_ORDER + o + 1) * HYENA_WIDTH]
                   * decs_ref[...])
            norm = (jnp.sum(jnp.abs(fwd), axis=0, keepdims=True)
                    + jnp.sum(jnp.abs(bwd), axis=0, keepdims=True))
            filt_ref[o] = fwd / norm
            filt_ref[HYENA_ORDER + o] = bwd / norm

    fwd = filt_ref[order]
    bwd = filt_ref[HYENA_ORDER + order]
    fh, fl, rh, rl = fh_ref[...], fl_ref[...], rh_ref[...], rl_ref[...]
    spec = {}
    for d in range(nblk):
        spec[d] = _dot_table(fh, fl, fwd[d * blk:(d + 1) * blk])
        spec[-(d + 1)] = _dot_table(rh, rl, bwd[d * blk:(d + 1) * blk])
    row = lax.broadcasted_iota(jnp.int32, (2 * blk, HYENA_WIDTH), 0)
    sign = (1 - 2 * (row & 1)).astype(F32)
    for d in range(-(nblk - 1), nblk):
        o_ref[0, d + nblk - 1] = spec[d] + sign * spec[d - 1]


def _filter_weights(w1, b1, w2, b2, w3, sf):
    fo = HYENA_FILTER_ORDER

    def block_diag(a, b):
        top = jnp.concatenate([a, jnp.zeros((a.shape[0], b.shape[1]), F32)], axis=1)
        bot = jnp.concatenate([jnp.zeros((b.shape[0], a.shape[1]), F32), b], axis=1)
        return jnp.concatenate([top, bot], axis=0)

    w1p = jnp.zeros((LANES, fo), F32).at[:HYENA_EMB_DIM].set(w1)
    w3r = w3.reshape(fo, HYENA_ORDER, 2, HYENA_WIDTH)
    w3f = w3r[:, :, 0].reshape(fo, HYENA_ORDER * HYENA_WIDTH)
    w3b = w3r[:, :, 1].reshape(fo, HYENA_ORDER * HYENA_WIDTH)
    return (block_diag(w1p, w1p), jnp.tile(b1, 2).reshape(1, 2 * fo),
            block_diag(w2, w2), jnp.tile(b2, 2).reshape(1, 2 * fo),
            block_diag(w3f, w3b), jnp.tile(sf, (1, 2)))


def _filter_call(tables, dft, weights, *, seq, blk):
    nblk = seq // blk
    zz, dec, decs = tables
    fh, fl, rh, rl = dft
    fo2 = 2 * HYENA_FILTER_ORDER
    n_taps = 2 * HYENA_ORDER * HYENA_WIDTH
    return pl.pallas_call(
        functools.partial(_filter_kernel, blk=blk, nblk=nblk),
        grid=(HYENA_ORDER,),
        in_specs=[
            _resident((seq, 2 * LANES)),
            _resident((seq, HYENA_WIDTH)), _resident((seq, HYENA_WIDTH)),
            _resident((2 * LANES, fo2)), _resident((1, fo2)),
            _resident((fo2, fo2)), _resident((1, fo2)),
            _resident((fo2, n_taps)),
            _resident((2, fo2)),
            _resident((2 * blk, blk)), _resident((2 * blk, blk)),
            _resident((2 * blk, blk)), _resident((2 * blk, blk)),
        ],
        out_specs=pl.BlockSpec((1, 2 * nblk - 1, 2 * blk, HYENA_WIDTH), lambda o: (o, 0, 0, 0)),
        out_shape=jax.ShapeDtypeStruct((HYENA_ORDER, 2 * nblk - 1, 2 * blk, HYENA_WIDTH), F32),
        scratch_shapes=[pltpu.VMEM((2 * HYENA_ORDER, seq, HYENA_WIDTH), F32)],
        compiler_params=_params(1),
        name="hyena_filter",
    )(zz, dec, decs, *weights, fh, fl, rh, rl)


def _short_conv3(p, cw, cb):
    seq, width = p.shape
    zc = pltpu.roll(p, 1, 0) * cw[0:1] + p * cw[1:2] + pltpu.roll(p, seq - 1, 0) * cw[2:3] + cb
    row = lax.broadcasted_iota(jnp.int32, (SUBLANES, width), 0)
    head = zc[:SUBLANES] - jnp.where(row == 0, p[seq - 1:seq] * cw[0:1], 0.0)
    tail = zc[seq - SUBLANES:] - jnp.where(row == SUBLANES - 1, p[0:1] * cw[2:3], 0.0)
    return jnp.concatenate([head, zc[SUBLANES:seq - SUBLANES], tail], axis=0)


def _hyena_kernel(p_ref, cw_ref, cb_ref, hb_ref, wf_ref, wi_ref, ch_ref, o_ref, *, seq, blk):
    nblk = seq // blk
    zc = _short_conv3(p_ref[0], cw_ref[...], cb_ref[...])
    y = zc[:, :HYENA_WIDTH]
    wf = wf_ref[...]
    wi = wi_ref[...]
    hb = hb_ref[...]
    first = lax.broadcasted_iota(jnp.int32, (blk, HYENA_WIDTH), 0) == 0
    for o in range(HYENA_ORDER):
        gate = zc[:, (o + 1) * HYENA_WIDTH:(o + 2) * HYENA_WIDTH]
        spec = [_dot(wf, y[j * blk:(j + 1) * blk].astype(BF16)) for j in range(nblk)]
        outs = []
        for i in range(nblk):
            acc_r = jnp.zeros((blk, HYENA_WIDTH), F32)
            acc_i = jnp.zeros((blk, HYENA_WIDTH), F32)
            for j in range(nblk):
                cm = ch_ref[o, i - j + nblk - 1]
                ur, ui = spec[j][:blk], spec[j][blk:]
                cr, ci = cm[:blk], cm[blk:]
                ii = ui * ci
                acc_r = acc_r + ur * cr - jnp.where(first, 0.0, ii)
                acc_i = acc_i + jnp.where(first, ii, ur * ci + ui * cr)
            prod = jnp.concatenate([acc_r, acc_i], axis=0).astype(BF16)
            outs.append(_dot(wi, prod))
        conv = outs[0] if nblk == 1 else jnp.concatenate(outs, axis=0)
        y = gate * (conv + hb[o:o + 1] * y)
    o_ref[0] = y.astype(BF16)


def _hyena_call(p, conv_w, conv_b, hy_bias, wf, wi, chat, *, blk):
    n_seq, lq, width = p.shape
    nblk = lq // blk
    return pl.pallas_call(
        functools.partial(_hyena_kernel, seq=lq, blk=blk),
        grid=(n_seq,),
        in_specs=[
            pl.BlockSpec((1, lq, width), lambda s: (s, 0, 0)),
            _resident((3, width)),
            _resident((1, width)),
            _resident((HYENA_ORDER, HYENA_WIDTH)),
            _resident((2 * blk, blk)),
            _resident((blk, 2 * blk)),
            _resident((HYENA_ORDER, 2 * nblk - 1, 2 * blk, HYENA_WIDTH)),
        ],
        out_specs=pl.BlockSpec((1, lq, HYENA_WIDTH), lambda s: (s, 0, 0)),
        out_shape=jax.ShapeDtypeStruct((n_seq, lq, HYENA_WIDTH), BF16),
        compiler_params=_params(1),
        name="hyena",
    )(p, conv_w, conv_b.reshape(1, width), hy_bias, wf, wi, chat)


def _qk(q, k):
    return lax.dot_general(q, k, (((1,), (1,)), ((), ())), preferred_element_type=F32)


def _attn_kernel(*refs, with_latent):
    if with_latent:
        q_ref, k_ref, v_ref, kc_ref, vc_ref, o_ref, keys_ref, vals_ref = refs
        sources = ((k_ref, v_ref), (kc_ref, vc_ref))
    else:
        q_ref, kc_ref, vc_ref, o_ref, keys_ref, vals_ref = refs
        sources = ((kc_ref, vc_ref),)

    @pl.when(pl.program_id(1) == 0)
    def _():
        lo = 0
        for key_ref, val_ref in sources:
            n = key_ref.shape[1]
            kk, vv = key_ref[0], val_ref[0]
            for kh in range(N_KV_HEADS):
                cols = slice(kh * HEAD_DIM, (kh + 1) * HEAD_DIM)
                keys_ref[kh, lo:lo + n, :] = kk[:, cols]
                vals_ref[kh, lo:lo + n, :] = jnp.concatenate(
                    [vv[:, cols], jnp.ones((n, HEAD_DIM), BF16)], axis=1)
            lo += n

    q = q_ref[0]
    tq = q.shape[0]
    sub = min(Q_SUB, tq)
    pieces = [[None] * N_Q_HEADS for _ in range(tq // sub)]
    for kh in range(N_KV_HEADS):
        k_h, v_h = keys_ref[kh], vals_ref[kh]
        for r in range(tq // sub):
            rows = slice(r * sub, (r + 1) * sub)
            qs = jnp.concatenate(
                [q[rows, (kh * GQA_GROUP + g) * HEAD_DIM:(kh * GQA_GROUP + g + 1) * HEAD_DIM]
                 for g in range(GQA_GROUP)], axis=0)
            s = _qk(qs, k_h)
            o = _dot(jnp.exp2(s - jnp.max(s, axis=-1, keepdims=True)).astype(BF16), v_h)
            o = o[:, :HEAD_DIM] / o[:, HEAD_DIM:]
            for g in range(GQA_GROUP):
                pieces[r][kh * GQA_GROUP + g] = o[g * sub:(g + 1) * sub]
    o_ref[0] = jnp.concatenate(
        [jnp.concatenate(row, axis=1) for row in pieces], axis=0).astype(BF16)


def _attn_call(q, kc, vc, k=None, v=None):
    n_seq, lq, _ = q.shape
    tq = min(Q_TILE, lq)
    ctx_len = kc.shape[1]
    with_latent = k is not None
    q_spec = pl.BlockSpec((1, tq, ATTN_WIDTH), lambda s, i: (s, i, 0))
    kv_spec = lambda n: pl.BlockSpec((1, n, KV_WIDTH), lambda s, i: (s, 0, 0))
    if with_latent:
        in_specs = [q_spec, kv_spec(lq), kv_spec(lq), kv_spec(ctx_len), kv_spec(ctx_len)]
        args = (q, k, v, kc, vc)
        n_keys = lq + ctx_len
    else:
        in_specs = [q_spec, kv_spec(ctx_len), kv_spec(ctx_len)]
        args = (q, kc, vc)
        n_keys = ctx_len
    return pl.pallas_call(
        functools.partial(_attn_kernel, with_latent=with_latent),
        grid=(n_seq, lq // tq),
        in_specs=in_specs,
        out_specs=q_spec,
        out_shape=jax.ShapeDtypeStruct((n_seq, lq, ATTN_WIDTH), BF16),
        scratch_shapes=[pltpu.VMEM((N_KV_HEADS, n_keys, HEAD_DIM), BF16),
                        pltpu.VMEM((N_KV_HEADS, n_keys, 2 * HEAD_DIM), BF16)],
        compiler_params=_params(2),
        name="attn" if with_latent else "attn_ctx",
    )(*args)


def _rope_tables(seq):
    rows = seq // GRID_W
    row = np.repeat(np.arange(rows), GRID_W).astype(np.float64)
    col = np.tile(np.arange(GRID_W), rows).astype(np.float64)
    inv_freq = 1.0 / (ROPE_THETA ** (np.arange(0, AXIS_DIM, 2, dtype=np.float64) / AXIS_DIM))
    ang_r = row[:, None] * inv_freq
    ang_c = col[:, None] * inv_freq
    ang = np.concatenate([ang_r, ang_r, ang_c, ang_c], axis=-1)
    cos, sin = np.cos(ang), np.sin(ang)
    upper = (np.arange(HEAD_DIM) % AXIS_DIM) >= AXIS_DIM // 2
    sin_a = np.where(upper, sin, 0.0)
    sin_b = np.where(upper, 0.0, -sin)
    rep = LANES // HEAD_DIM
    return tuple(jnp.asarray(np.tile(a, (1, rep)).astype(np.float32)) for a in (cos, sin_a, sin_b))


def _identity_tables(seq):
    one = jnp.ones((seq, LANES), F32)
    zero = jnp.zeros((seq, LANES), F32)
    return one, zero, zero


def _dft_operands(blk):
    fwd, rev, inv = (jnp.asarray(a) for a in _dft_tables(blk))
    fh, fl = _split_bf16(fwd)
    rh, rl = _split_bf16(rev)
    return (fh, fl, rh, rl), fh, inv.astype(BF16)


def kernel(x, c, ctx, c_ctx, norm_g, w_mod, b_mod, ffn_w_gate, ffn_w_up, ffn_w_down, w_in, w_out, pool_w, pool_scale, hyena_conv_w, hyena_conv_b, hyena_f_w1, hyena_f_b1, hyena_f_w2, hyena_f_b2, hyena_f_w3, hyena_sin_freq, hyena_bias, q_norm_g, k_norm_g):
    batch, seq, d = x.shape
    ctx_len = ctx.shape[1]
    depth = w_mod.shape[0]

    rows = -(-(batch + 1) // SUBLANES) * SUBLANES
    cvec = jnp.zeros((rows, d), F32).at[:batch].set(c).at[batch].set(c_ctx)
    mods = _mods_call(cvec, w_mod, b_mod).reshape(depth * rows, N_MOD, d)

    blk_x = min(DFT_BLOCK, seq)
    blk_c = min(DFT_BLOCK, ctx_len)
    dft_x = _dft_operands(blk_x)
    dft_c = _dft_operands(blk_c)
    hy_tab_x = tuple(jnp.asarray(a) for a in _hyena_tables(seq))
    hy_tab_c = tuple(jnp.asarray(a) for a in _hyena_tables(ctx_len))
    rope_x = _rope_tables(seq)
    rope_c = _identity_tables(batch * ctx_len)
    seg = np.arange(LANES) // HEAD_DIM
    seg_mean = jnp.asarray((seg[:, None] == seg[None, :]).astype(np.float32) / HEAD_DIM).astype(BF16)
    n_win = len(POOL_WINDOWS)

    ffn_w = tuple(w.astype(BF16) for w in (ffn_w_gate, ffn_w_up, ffn_w_down))
    w_in_b = w_in.astype(BF16)
    w_out_b = w_out.astype(BF16)

    h, hc = x, ctx.reshape(1, batch * ctx_len, d)
    for l in range(depth):
        last = l == depth - 1
        row_x = lambda s, l=l: l * rows + s
        row_c = lambda s, l=l: l * rows + batch
        w_pool = jnp.zeros((n_win, POOL_GROUP, n_win, POOL_GROUP), F32)
        for gi in range(n_win):
            w_pool = w_pool.at[gi, :, gi, :].set(pool_w[l, gi])
        w_pool = w_pool.reshape(POOL_WIDTH, POOL_WIDTH).astype(BF16)
        qg = jnp.tile(q_norm_g[l], LANES // HEAD_DIM).reshape(1, LANES)
        kg = jnp.tile(k_norm_g[l], LANES // HEAD_DIM).reshape(1, LANES)
        filt_w = _filter_weights(hyena_f_w1[l], hyena_f_b1[l], hyena_f_w2[l], hyena_f_b2[l],
                                 hyena_f_w3[l], hyena_sin_freq[l])

        h, pool_x, hy_x, q_x, k_x, v_x = _ffn_in_call(
            h, mods, norm_g[l, 0], ffn_w, norm_g[l, 1], w_in_b, rope_x, qg, kg, seg_mean,
            mod_row=row_x, layer=l, half=0)
        hc, *proj_c = _ffn_in_call(
            hc, mods, norm_g[l, 0], ffn_w, norm_g[l, 1], w_in_b, rope_c, qg, kg, seg_mean,
            mod_row=row_c, layer=l, half=0)
        pool_c, hy_c, q_c, k_c, v_c = (a.reshape(batch, ctx_len, a.shape[-1]) for a in proj_c)

        chat_x = _filter_call(hy_tab_x, dft_x[0], filt_w, seq=seq, blk=blk_x)
        mix_x = (
            _pool_call(pool_x, w_pool, pool_scale[l]),
            _hyena_call(hy_x, hyena_conv_w[l], hyena_conv_b[l], hyena_bias[l], dft_x[1], dft_x[2],
                        chat_x, blk=blk_x),
            _attn_call(q_x, k_c, v_c, k_x, v_x),
        )
        if not last:
            chat_c = _filter_call(hy_tab_c, dft_c[0], filt_w, seq=ctx_len, blk=blk_c)
            mix_c = (
                _pool_call(pool_c, w_pool, pool_scale[l]),
                _hyena_call(hy_c, hyena_conv_w[l], hyena_conv_b[l], hyena_bias[l], dft_c[1], dft_c[2],
                            chat_c, blk=blk_c),
                _attn_call(q_c, k_c, v_c),
            )
            mix_c = tuple(a.reshape(1, batch * ctx_len, a.shape[-1]) for a in mix_c)
            hc = _ffn_out_call(hc, mods, mix_c, w_out_b, norm_g[l, 2], ffn_w,
                               mod_row=row_c, layer=l, half=1)
        h = _ffn_out_call(h, mods, mix_x, w_out_b, norm_g[l, 2], ffn_w,
                          mod_row=row_x, layer=l, half=1)
    return h
```
